```python
import math
import jax
import jax.numpy as jnp
from jax import lax

D_MODEL = 1024
BATCH = 8
SEQ = 4096
DEPTH = 4

MEM_LEN = 256
MAX_POS_OFFSET = 4096

DN_HEADS = 4
DN_HEAD_DIM = 128
DN_KEY_DIM = DN_HEADS * DN_HEAD_DIM
DN_QKV_DIM = 3 * DN_KEY_DIM
DN_CONV = 4
DN_CHUNK = 64

SW_HEADS = 8
SW_HEAD_DIM = 64
SW_DIM = SW_HEADS * SW_HEAD_DIM
SW_BRANCHES = ((128, 1), (512, 4), (2048, 16))
SW_BLOCK = 128
ROPE_THETA = 10000.0

IN_SPLITS = (DN_QKV_DIM, DN_KEY_DIM, DN_HEADS, DN_HEADS, SW_DIM, SW_DIM, SW_DIM)
HYB_IN = sum(IN_SPLITS)
HYB_MIX = DN_KEY_DIM + SW_DIM

S5_GROUP = 16
S5_GROUPS = D_MODEL // S5_GROUP
S5_STATE = 64

X_HEADS = 4
X_HEAD_DIM = D_MODEL // X_HEADS

FFN_HIDDEN = -(-8 * D_MODEL // (3 * 256)) * 256

N_EVEN = (DEPTH + 1) // 2
N_ODD = DEPTH // 2
DEEPNORM_ALPHA = (2 * DEPTH) ** 0.25
DEEPNORM_BETA = (8 * DEPTH) ** -0.25
LN_EPS = 1e-5
RMS_EPS = 1e-6

kernel_name = 'hybrid_deltanet_dilated_s5_block'


def _split_points(sizes):
    pts, acc = [], 0
    for s in sizes[:-1]:
        acc += s
        pts.append(acc)
    return pts


def layer_norm(x, g, b):
    xf = x.astype(jnp.float32)
    mu = xf.mean(-1, keepdims=True)
    var = jnp.square(xf - mu).mean(-1, keepdims=True)
    return (xf - mu) * lax.rsqrt(var + LN_EPS) * g.astype(jnp.float32) + b.astype(jnp.float32)


def post_norm(h, sub, g, b):
    return layer_norm(DEEPNORM_ALPHA * h.astype(jnp.float32) + sub.astype(jnp.float32), g, b).astype(h.dtype)


def rms_norm(x, g):
    xf = x.astype(jnp.float32)
    return xf * lax.rsqrt(jnp.mean(xf * xf, -1, keepdims=True) + RMS_EPS) * g.astype(jnp.float32)


def l2_normalize(x):
    xf = x.astype(jnp.float32)
    return xf * lax.rsqrt(jnp.sum(xf * xf, -1, keepdims=True) + RMS_EPS)


def causal_depthwise_conv(x, w):
    k, c = w.shape
    return lax.conv_general_dilated(
        x, w.astype(x.dtype)[:, None, :], window_strides=(1,), padding=[(k - 1, 0)],
        dimension_numbers=('NWC', 'WIO', 'NWC'), feature_group_count=c)


def rope(x, pos):
    d = x.shape[-1]
    inv_freq = ROPE_THETA ** (-jnp.arange(0, d, 2, dtype=jnp.float32) / d)
    ang = pos.astype(jnp.float32)[..., None] * inv_freq
    cos, sin = jnp.cos(ang)[:, :, None, :], jnp.sin(ang)[:, :, None, :]
    xf = x.astype(jnp.float32)
    x1, x2 = xf[..., : d // 2], xf[..., d // 2:]
    return jnp.concatenate([x1 * cos - x2 * sin, x2 * cos + x1 * sin], -1).astype(x.dtype)


def gated_delta_rule(q, k, v, g, beta):
    bsz, s, h, dk = q.shape
    dv = v.shape[-1]
    c = DN_CHUNK
    n = s // c

    def chunk(t):
        return t.reshape(bsz, n, c, h, -1).transpose(0, 3, 1, 2, 4)

    q = chunk(q) * (dk ** -0.5)
    k = chunk(k)
    v = chunk(v)
    g = jnp.cumsum(chunk(g[..., None])[..., 0], axis=-1)
    beta = chunk(beta[..., None])[..., 0]
    causal = jnp.tril(jnp.ones((c, c), dtype=bool))
    strict = jnp.tril(jnp.ones((c, c), dtype=bool), -1)
    decay = jnp.exp(jnp.where(causal, g[..., :, None] - g[..., None, :], -jnp.inf))
    k_beta = k * beta[..., None]
    lower = jnp.where(strict, jnp.einsum('bhnid,bhnjd->bhnij', k_beta, k), 0.0) * decay
    eye = jnp.eye(c, dtype=jnp.float32)
    rhs = jnp.concatenate([v * beta[..., None], k_beta * jnp.exp(g)[..., None]], axis=-1)
    sol = lax.linalg.triangular_solve(lower + eye, rhs, left_side=True, lower=True,
                                      unit_diagonal=True)
    u, w = sol[..., :dv], sol[..., dv:]
    intra = jnp.einsum('bhnid,bhnjd->bhnij', q, k) * decay
    q_dec = q * jnp.exp(g)[..., None]
    g_last = g[..., -1]
    k_dec = k * jnp.exp(g_last[..., None] - g)[..., None]

    def step(state, inp):
        q_c, w_c, u_c, k_c, a_c, gl = inp
        v_new = u_c - jnp.einsum('bhcd,bhde->bhce', w_c, state)
        out = (jnp.einsum('bhcd,bhde->bhce', q_c, state)
               + jnp.einsum('bhij,bhje->bhie', a_c, v_new))
        state = state * jnp.exp(gl)[..., None, None] + jnp.einsum('bhcd,bhce->bhde', k_c, v_new)
        return state, out

    xs = tuple(jnp.moveaxis(t, 2, 0) for t in (q_dec, w, u, k_dec, intra, g_last))
    state0 = jnp.zeros((bsz, h, dk, dv), jnp.float32)
    _, out = lax.scan(step, state0, xs)
    return out.transpose(1, 0, 3, 2, 4).reshape(bsz, s, h, dv)


def gated_deltanet(qkv, z, b_logit, a_logit, conv_w, a_log, dt_bias, norm_g):
    bsz, s, _ = qkv.shape
    qkv = jax.nn.silu(causal_depthwise_conv(qkv, conv_w))
    q, k, v = jnp.split(qkv, [DN_KEY_DIM, 2 * DN_KEY_DIM], axis=-1)
    heads = lambda t: t.reshape(bsz, s, DN_HEADS, DN_HEAD_DIM)
    q = l2_normalize(heads(q))
    k = l2_normalize(heads(k))
    v = heads(v).astype(jnp.float32)
    beta = jax.nn.sigmoid(b_logit.astype(jnp.float32))
    g = -jnp.exp(a_log.astype(jnp.float32)) * jax.nn.softplus(
        a_logit.astype(jnp.float32) + dt_bias.astype(jnp.float32))
    o = gated_delta_rule(q, k, v, g, beta)
    o = rms_norm(o, norm_g) * jax.nn.silu(heads(z).astype(jnp.float32))
    return o.reshape(bsz, s, DN_KEY_DIM)


def dilated_branch(q, k, v, window, dilation):
    bsz, s, h, d = q.shape
    steps = window // dilation
    span = dilation * SW_BLOCK
    s_pad = -(-s // span) * span
    length = s_pad // dilation
    nb = length // SW_BLOCK

    def to_blocks(t):
        t = jnp.pad(t, ((0, 0), (0, s_pad - s), (0, 0), (0, 0)))
        t = t.reshape(bsz, length, dilation, h, -1).transpose(0, 2, 1, 3, 4)
        return t.reshape(bsz, dilation, nb, SW_BLOCK, h, -1)

    def with_prev(t):
        prev = jnp.pad(t, ((0, 0), (0, 0), (1, 0), (0, 0), (0, 0), (0, 0)))[:, :, :-1]
        return jnp.concatenate([prev, t], axis=3)

    def from_blocks(t):
        t = t.reshape(bsz, dilation, length, h, -1).transpose(0, 2, 1, 3, 4)
        return t.reshape(bsz, s_pad, h, -1)[:, :s]

    qb = to_blocks(q)
    kw = with_prev(to_blocks(k))
    vw = with_prev(to_blocks(v))
    scores = jnp.einsum('brnqhd,brnkhd->brnhqk', qb, kw,
                        preferred_element_type=jnp.float32) * (d ** -0.5)
    qi = jnp.arange(SW_BLOCK)[:, None] + SW_BLOCK
    kj = jnp.arange(2 * SW_BLOCK)[None, :]
    dist = qi - kj
    blk = jnp.arange(nb)[:, None, None]
    valid = (dist >= 0) & (dist <= steps) & (blk * SW_BLOCK + kj - SW_BLOCK >= 0)
    scores = jnp.where(valid[None, None, :, None], scores, -jnp.inf)
    m = scores.max(-1, keepdims=True)
    p = jnp.exp(scores - m)
    l = p.sum(-1)
    o = jnp.einsum('brnhqk,brnkhd->brnqhd', p, vw.astype(jnp.float32))
    o = o / jnp.swapaxes(l, -1, -2)[..., None]
    lse = jnp.swapaxes(m[..., 0] + jnp.log(l), -1, -2)
    return from_blocks(o), from_blocks(lse[..., None])[..., 0]


def dilated_attention(q, k, v, pos):
    q = rope(q, pos)
    k = rope(k, pos)
    outs, lses = [], []
    for window, dilation in SW_BRANCHES:
        o, lse = dilated_branch(q, k, v, window, dilation)
        outs.append(o)
        lses.append(lse)
    wts = jax.nn.softmax(jnp.stack(lses), axis=0)[..., None]
    return jnp.sum(jnp.stack(outs) * wts, axis=0)


def delta_dilated_mixer(h, positions, w_in, conv_w, a_log, dt_bias, norm_g, w_out):
    bsz, s, _ = h.shape
    proj = h @ w_in
    dn_qkv, dn_z, dn_b, dn_a, sw_q, sw_k, sw_v = jnp.split(proj, _split_points(IN_SPLITS), axis=-1)
    a_out = gated_deltanet(dn_qkv, dn_z, dn_b, dn_a, conv_w, a_log, dt_bias, norm_g)
    heads = lambda t: t.reshape(bsz, s, SW_HEADS, SW_HEAD_DIM)
    b_out = dilated_attention(heads(sw_q), heads(sw_k), heads(sw_v), positions).reshape(bsz, s, SW_DIM)
    mixed = jnp.concatenate([a_out, b_out], axis=-1).astype(h.dtype)
    return mixed @ w_out


def _lin_rec_combine(e1, e2):
    a1, b1 = e1
    a2, b2 = e2
    return a1 * a2, a2 * b1 + b2


def s5_mixer(u, a_re, a_im, log_dt, b_re, b_im, c_re, c_im, d_skip, w_o, w_g):
    bsz, s, d = u.shape
    uf = u.astype(jnp.float32)
    ug = uf.reshape(bsz, s, S5_GROUPS, S5_GROUP)
    f32 = jnp.float32
    a = lax.complex(a_re.astype(f32), a_im.astype(f32))
    dt = jnp.exp(log_dt.astype(f32))[:, None]
    a_bar = jnp.exp(a * dt)
    b_bar = ((a_bar - 1.0) / a)[..., None] * lax.complex(b_re.astype(f32), b_im.astype(f32))
    bu = jnp.einsum('gph,bsgh->bsgp', b_bar, ug.astype(jnp.complex64))
    a_seq = jnp.broadcast_to(a_bar, (1, s) + a_bar.shape)
    _, states = lax.associative_scan(_lin_rec_combine, (a_seq, bu), axis=1)
    y = (jnp.einsum('ghp,bsgp->bsgh', c_re.astype(f32), states.real)
         - jnp.einsum('ghp,bsgp->bsgh', c_im.astype(f32), states.imag))
    y = y.reshape(bsz, s, d) + d_skip.astype(f32) * uf
    hid = jax.nn.gelu(y).astype(u.dtype)
    return (hid @ w_o) * jax.nn.sigmoid(hid @ w_g)


def memory_cross_attention(h, mem, wq, wk, wv, wo):
    bsz, s, _ = h.shape
    m = mem.shape[1]
    q = (h @ wq).reshape(bsz, s, X_HEADS, X_HEAD_DIM)
    k = (mem @ wk).reshape(bsz, m, X_HEADS, X_HEAD_DIM)
    v = (mem @ wv).reshape(bsz, m, X_HEADS, X_HEAD_DIM)
    scores = jnp.einsum('bqhd,bkhd->bhqk', q, k, preferred_element_type=jnp.float32) * (X_HEAD_DIM ** -0.5)
    p = jax.nn.softmax(scores, axis=-1)
    o = jnp.einsum('bhqk,bkhd->bqhd', p, v.astype(jnp.float32)).reshape(bsz, s, D_MODEL)
    return o.astype(h.dtype) @ wo


def swiglu(h, wg, wu, wd):
    return (jax.nn.silu(h @ wg) * (h @ wu)) @ wd


def setup_inputs(seed: int = 0) -> dict:
    key = jax.random.key(seed)
    keys = iter(jax.random.split(key, 48))
    f32 = jnp.float32

    def nrm(shape, scale):
        return jax.random.normal(next(keys), shape, f32) * scale

    def uni(shape, lo, hi):
        return jax.random.uniform(next(keys), shape, f32, lo, hi)

    def gain(shape):
        return 1.0 + nrm(shape, 0.02)

    x = nrm((BATCH, SEQ, D_MODEL), 1.0)
    mem = nrm((BATCH, MEM_LEN, D_MODEL), 1.0)
    positions = (jax.random.randint(next(keys), (BATCH, 1), 0, MAX_POS_OFFSET, dtype=jnp.int32)
                 + jnp.arange(SEQ, dtype=jnp.int32)[None, :])

    hyb_w_in = nrm((N_EVEN, D_MODEL, HYB_IN), D_MODEL ** -0.5)
    dn_conv_w = nrm((N_EVEN, DN_CONV, DN_QKV_DIM), DN_CONV ** -0.5)
    dn_a_log = jnp.log(uni((N_EVEN, DN_HEADS), 1.0, 16.0))
    dt = jnp.exp(uni((N_EVEN, DN_HEADS), math.log(1e-3), math.log(1e-1)))
    dn_dt_bias = dt + jnp.log(-jnp.expm1(-dt))
    dn_norm_g = gain((N_EVEN, DN_HEAD_DIM))
    hyb_w_out = nrm((N_EVEN, HYB_MIX, D_MODEL), HYB_MIX ** -0.5 * DEEPNORM_BETA)

    s5_a_re = -0.5 + nrm((N_ODD, S5_GROUPS, S5_STATE), 0.01)
    s5_a_im = math.pi * jnp.arange(S5_STATE, dtype=f32) + nrm((N_ODD, S5_GROUPS, S5_STATE), 0.01)
    s5_log_dt = uni((N_ODD, S5_GROUPS), math.log(1e-3), math.log(1e-1))
    s5_b_re = nrm((N_ODD, S5_GROUPS, S5_STATE, S5_GROUP), (2 * S5_GROUP) ** -0.5)
    s5_b_im = nrm((N_ODD, S5_GROUPS, S5_STATE, S5_GROUP), (2 * S5_GROUP) ** -0.5)
    s5_c_re = nrm((N_ODD, S5_GROUPS, S5_GROUP, S5_STATE), 0.5 ** 0.5)
    s5_c_im = nrm((N_ODD, S5_GROUPS, S5_GROUP, S5_STATE), 0.5 ** 0.5)
    s5_d = nrm((N_ODD, D_MODEL), 1.0)
    s5_glu_wo = nrm((N_ODD, D_MODEL, D_MODEL), D_MODEL ** -0.5 * DEEPNORM_BETA)
    s5_glu_wg = nrm((N_ODD, D_MODEL, D_MODEL), D_MODEL ** -0.5)

    ln_mix_g = gain((DEPTH, D_MODEL))
    ln_mix_b = nrm((DEPTH, D_MODEL), 0.02)
    xq_w = nrm((DEPTH, D_MODEL, D_MODEL), D_MODEL ** -0.5)
    xk_w = nrm((DEPTH, D_MODEL, D_MODEL), D_MODEL ** -0.5)
    xv_w = nrm((DEPTH, D_MODEL, D_MODEL), D_MODEL ** -0.5)
    xo_w = nrm((DEPTH, D_MODEL, D_MODEL), D_MODEL ** -0.5 * DEEPNORM_BETA)
    ln_x_g = gain((DEPTH, D_MODEL))
    ln_x_b = nrm((DEPTH, D_MODEL), 0.02)
    ffn_wg = nrm((DEPTH, D_MODEL, FFN_HIDDEN), D_MODEL ** -0.5)
    ffn_wu = nrm((DEPTH, D_MODEL, FFN_HIDDEN), D_MODEL ** -0.5)
    ffn_wd = nrm((DEPTH, FFN_HIDDEN, D_MODEL), FFN_HIDDEN ** -0.5 * DEEPNORM_BETA)
    ln_ffn_g = gain((DEPTH, D_MODEL))
    ln_ffn_b = nrm((DEPTH, D_MODEL), 0.02)

    return {
        'x': x, 'mem': mem, 'positions': positions,
        'hyb_w_in': hyb_w_in, 'dn_conv_w': dn_conv_w, 'dn_a_log': dn_a_log,
        'dn_dt_bias': dn_dt_bias, 'dn_norm_g': dn_norm_g, 'hyb_w_out': hyb_w_out,
        's5_a_re': s5_a_re, 's5_a_im': s5_a_im, 's5_log_dt': s5_log_dt,
        's5_b_re': s5_b_re, 's5_b_im': s5_b_im, 's5_c_re': s5_c_re, 's5_c_im': s5_c_im,
        's5_d': s5_d, 's5_glu_wo': s5_glu_wo, 's5_glu_wg': s5_glu_wg,
        'ln_mix_g': ln_mix_g, 'ln_mix_b': ln_mix_b,
        'xq_w': xq_w, 'xk_w': xk_w, 'xv_w': xv_w, 'xo_w': xo_w,
        'ln_x_g': ln_x_g, 'ln_x_b': ln_x_b,
        'ffn_wg': ffn_wg, 'ffn_wu': ffn_wu, 'ffn_wd': ffn_wd,
        'ln_ffn_g': ln_ffn_g, 'ln_ffn_b': ln_ffn_b,
    }


def reference(x, mem, positions,
              hyb_w_in, dn_conv_w, dn_a_log, dn_dt_bias, dn_norm_g, hyb_w_out,
              s5_a_re, s5_a_im, s5_log_dt, s5_b_re, s5_b_im, s5_c_re, s5_c_im,
              s5_d, s5_glu_wo, s5_glu_wg,
              ln_mix_g, ln_mix_b,
              xq_w, xk_w, xv_w, xo_w, ln_x_g, ln_x_b,
              ffn_wg, ffn_wu, ffn_wd, ln_ffn_g, ln_ffn_b):
    h = x
    for layer in range(DEPTH):
        i = layer // 2
        if layer % 2 == 0:
            mix = delta_dilated_mixer(h, positions, hyb_w_in[i], dn_conv_w[i], dn_a_log[i],
                                      dn_dt_bias[i], dn_norm_g[i], hyb_w_out[i])
        else:
            mix = s5_mixer(h, s5_a_re[i], s5_a_im[i], s5_log_dt[i], s5_b_re[i], s5_b_im[i],
                           s5_c_re[i], s5_c_im[i], s5_d[i], s5_glu_wo[i], s5_glu_wg[i])
        h = post_norm(h, mix, ln_mix_g[layer], ln_mix_b[layer])
        h = post_norm(h, memory_cross_attention(h, mem, xq_w[layer], xk_w[layer], xv_w[layer], xo_w[layer]),
                      ln_x_g[layer], ln_x_b[layer])
        h = post_norm(h, swiglu(h, ffn_wg[layer], ffn_wu[layer], ffn_wd[layer]),
                      ln_ffn_g[layer], ln_ffn_b[layer])
    return h
```

```python
import functools
import math

import jax
import jax.numpy as jnp
from jax import lax
from jax.experimental import pallas as pl
from jax.experimental.pallas import tpu as pltpu

F32 = jnp.float32
BF16 = jnp.bfloat16

D_MODEL = 1024
DEPTH = 4
DN_HEADS = 4
DN_HEAD_DIM = 128
DN_KEY_DIM = DN_HEADS * DN_HEAD_DIM
DN_QKV_DIM = 3 * DN_KEY_DIM
DN_CONV = 4
DN_CHUNK = 64
SW_HEADS = 8
SW_HEAD_DIM = 64
SW_DIM = SW_HEADS * SW_HEAD_DIM
SW_BRANCHES = ((128, 1), (512, 4), (2048, 16))
SW_BLOCK = 128
ROPE_THETA = 10000.0
S5_GROUP = 16
S5_GROUPS = D_MODEL // S5_GROUP
S5_STATE = 64
S5_CHUNK = 32
X_HEADS = 4
X_HEAD_DIM = D_MODEL // X_HEADS
DEEPNORM_ALPHA = (2 * DEPTH) ** 0.25
LN_EPS = 1e-5
RMS_EPS = 1e-6

LANES = 128
GATE_PAD = LANES
VMEM_LIMIT = 56 * 1024 * 1024
NEG_BIG = -1e30

TM = 512
DN_TILE = 256


def _cparams(*sem):
    return pltpu.CompilerParams(dimension_semantics=sem, vmem_limit_bytes=VMEM_LIMIT)


def _const_spec(shape):
    nd = len(shape)
    return pl.BlockSpec(shape, lambda *_: (0,) * nd, pipeline_mode=pl.Buffered(1))


def _dot(a, b):
    return jnp.dot(a, b, preferred_element_type=F32)


def _dot_nt(a, b):
    return lax.dot_general(a, b, (((1,), (1,)), ((), ())), preferred_element_type=F32)


def _dot_tn(a, b):
    return lax.dot_general(a, b, (((0,), (0,)), ((), ())), preferred_element_type=F32)


def _post_norm(h, sub, g, b):
    y = DEEPNORM_ALPHA * h + sub
    mu = jnp.mean(y, axis=-1, keepdims=True)
    yc = y - mu
    var = jnp.mean(yc * yc, axis=-1, keepdims=True)
    return yc * lax.rsqrt(var + LN_EPS) * g + b


def _sigmoid(x):
    return 1.0 / (1.0 + jnp.exp(-x))


def _silu(x):
    return x * _sigmoid(x)


def _ffn_body(h_ref, wg_ref, wu_ref, wd_ref, g_ref, b_ref, o_ref, *, n_chunks, fc):
    x = h_ref[...]
    xb = x.astype(BF16)
    acc = jnp.zeros(x.shape, F32)
    for c in range(n_chunks):
        lo, hi = c * fc, (c + 1) * fc
        gate = _dot(xb, wg_ref[:, lo:hi])
        up = _dot(xb, wu_ref[:, lo:hi])
        act = (_silu(gate) * up).astype(BF16)
        acc = acc + _dot(act, wd_ref[lo:hi, :])
    o_ref[...] = _post_norm(x, acc, g_ref[...], b_ref[...])


def _ffn(h2, wg, wu, wd, g, b):
    t, d = h2.shape
    f = wg.shape[1]
    n_chunks = 2
    fc = f // n_chunks
    tm = min(TM, t)
    return pl.pallas_call(
        functools.partial(_ffn_body, n_chunks=n_chunks, fc=fc),
        grid=(t // tm,),
        in_specs=[pl.BlockSpec((tm, d), lambda i: (i, 0)),
                  _const_spec((d, f)), _const_spec((d, f)), _const_spec((f, d)),
                  _const_spec((1, d)), _const_spec((1, d))],
        out_specs=pl.BlockSpec((tm, d), lambda i: (i, 0)),
        out_shape=jax.ShapeDtypeStruct((t, d), F32),
        compiler_params=_cparams("parallel"),
        name="swiglu_postnorm",
    )(h2, wg, wu, wd, g, b)


def _mm_body(x_ref, w_ref, o_ref):
    o_ref[...] = _dot(x_ref[...].astype(BF16), w_ref[...]).astype(o_ref.dtype)


def _mem_kv(mem2, wkv_all):
    r, d = mem2.shape
    n = wkv_all.shape[1]
    nb = 2 * d
    tm = min(TM, r)
    return pl.pallas_call(
        _mm_body,
        grid=(n // nb, r // tm),
        in_specs=[pl.BlockSpec((tm, d), lambda j, i: (i, 0)),
                  pl.BlockSpec((d, nb), lambda j, i: (0, j))],
        out_specs=pl.BlockSpec((tm, nb), lambda j, i: (i, j)),
        out_shape=jax.ShapeDtypeStruct((r, n), BF16),
        compiler_params=_cparams("parallel", "parallel"),
        name="memory_kv_proj",
    )(mem2, wkv_all)


def _xattn_body(h_ref, kv_ref, wq_ref, wo_ref, g_ref, b_ref, o_ref):
    x = h_ref[0]
    d = x.shape[-1]
    q = _dot(x.astype(BF16), wq_ref[...]) * (X_HEAD_DIM ** -0.5)
    qb = q.astype(BF16)
    outs = []
    for hh in range(X_HEADS):
        lo, hi = hh * X_HEAD_DIM, (hh + 1) * X_HEAD_DIM
        k = kv_ref[0, :, lo:hi]
        v = kv_ref[0, :, d + lo:d + hi]
        s = _dot_nt(qb[:, lo:hi], k)
        m = jnp.max(s, axis=-1, keepdims=True)
        p = jnp.exp(s - m)
        l = jnp.sum(p, axis=-1, keepdims=True)
        outs.append(_dot(p.astype(BF16), v) / l)
    o = jnp.concatenate(outs, axis=-1).astype(BF16)
    o_ref[0] = _post_norm(x, _dot(o, wo_ref[...]), g_ref[...], b_ref[...])


def _xattn(h, kv_all, layer, wq, wo, g, b):
    bsz, s, d = h.shape
    m = kv_all.shape[1]
    tm = min(TM, s)
    return pl.pallas_call(
        _xattn_body,
        grid=(bsz, s // tm),
        in_specs=[pl.BlockSpec((1, tm, d), lambda bi, i: (bi, i, 0)),
                  pl.BlockSpec((1, m, 2 * d), lambda bi, i: (bi, 0, layer)),
                  _const_spec((d, d)), _const_spec((d, d)),
                  _const_spec((1, d)), _const_spec((1, d))],
        out_specs=pl.BlockSpec((1, tm, d), lambda bi, i: (bi, i, 0)),
        out_shape=jax.ShapeDtypeStruct((bsz, s, d), F32),
        compiler_params=_cparams("parallel", "parallel"),
        name="memory_xattn_postnorm",
    )(h, kv_all, wq, wo, g, b)


def _rope_table_body(pos_ref, invf_ref, sgn_ref, cos_ref, sin_ref):
    ang = pos_ref[0] * invf_ref[...]
    cos_ref[0] = jnp.cos(ang)
    sin_ref[0] = jnp.sin(ang) * sgn_ref[...]


def _rope_tables(positions):
    bsz, s = positions.shape
    half = SW_HEAD_DIM // 2
    inv_freq = ROPE_THETA ** (-jnp.arange(0, SW_HEAD_DIM, 2, dtype=F32) / SW_HEAD_DIM)
    invf = jnp.tile(inv_freq, LANES // half)[None, :]
    sgn = jnp.tile(jnp.concatenate([-jnp.ones((half,), F32), jnp.ones((half,), F32)]),
                   LANES // SW_HEAD_DIM)[None, :]
    posf = positions.astype(F32)[..., None]
    tm = min(TM, s)
    return pl.pallas_call(
        _rope_table_body,
        grid=(bsz, s // tm),
        in_specs=[pl.BlockSpec((1, tm, 1), lambda bi, i: (bi, i, 0)),
                  _const_spec((1, LANES)), _const_spec((1, LANES))],
        out_specs=[pl.BlockSpec((1, tm, LANES), lambda bi, i: (bi, i, 0))] * 2,
        out_shape=[jax.ShapeDtypeStruct((bsz, s, LANES), F32)] * 2,
        compiler_params=_cparams("parallel", "parallel"),
        name="rope_tables",
    )(posf, invf, sgn)


_DNZ = DN_QKV_DIM + DN_KEY_DIM
_SW_OFF = _DNZ + GATE_PAD
_HYB_PACKED = _SW_OFF + 3 * SW_DIM


def _hyb_in_body(h_ref, w_ref, cos_ref, sin_ref, dn_ref, gate_ref, q_ref, k_ref, v_ref):
    x = h_ref[0].astype(BF16)
    dn_ref[0] = _dot(x, w_ref[:, :_DNZ]).astype(dn_ref.dtype)
    gate_ref[0] = _dot(x, w_ref[:, _DNZ:_SW_OFF])
    cos = cos_ref[0]
    sin = sin_ref[0]
    lane = lax.broadcasted_iota(jnp.int32, cos.shape, 1)
    first_half = (lane % SW_HEAD_DIM) < (SW_HEAD_DIM // 2)
    half = SW_HEAD_DIM // 2
    for idx, out_ref in enumerate((q_ref, k_ref, v_ref)):
        lo = _SW_OFF + idx * SW_DIM
        y = _dot(x, w_ref[:, lo:lo + SW_DIM])
        for hp in range(SW_DIM // LANES):
            yy = y[:, hp * LANES:(hp + 1) * LANES]
            if idx < 2:
                swapped = jnp.where(first_half, pltpu.roll(yy, LANES - half, 1), pltpu.roll(yy, half, 1))
                yy = yy * cos + swapped * sin
            out_ref[0, hp] = yy


def _hyb_in(h, w_packed, cos, sin):
    bsz, s, d = h.shape
    tm = min(TM, s)
    nhp = SW_DIM // LANES
    qkv_shape = jax.ShapeDtypeStruct((bsz, nhp, s, LANES), F32)
    qkv_spec = pl.BlockSpec((1, nhp, tm, LANES), lambda bi, i: (bi, 0, i, 0))
    tok = lambda w: pl.BlockSpec((1, tm, w), lambda bi, i: (bi, i, 0))
    return pl.pallas_call(
        _hyb_in_body,
        grid=(bsz, s // tm),
        in_specs=[tok(d), _const_spec((d, _HYB_PACKED)), tok(LANES), tok(LANES)],
        out_specs=[tok(_DNZ), tok(GATE_PAD), qkv_spec, qkv_spec, qkv_spec],
        out_shape=[jax.ShapeDtypeStruct((bsz, s, _DNZ), BF16),
                   jax.ShapeDtypeStruct((bsz, s, GATE_PAD), F32),
                   qkv_shape, qkv_shape, qkv_shape],
        compiler_params=_cparams("parallel", "parallel"),
        name="hybrid_in_proj_rope",
    )(h, w_packed, cos, sin)


def _l2n(x):
    return x * lax.rsqrt(jnp.sum(x * x, axis=-1, keepdims=True) + RMS_EPS)


def _deltanet_body(dn_ref, gate_ref, convw_ref, alog_ref, dtb_ref, ng_ref, o_ref, xbuf, state, *, ts):
    i = pl.program_id(1)
    c = DN_CHUNK
    nchunk = ts // c
    hd = DN_HEAD_DIM

    @pl.when(i == 0)
    def _():
        xbuf[0:8, :] = jnp.zeros((8, DN_QKV_DIM), F32)
        state[...] = jnp.zeros(state.shape, F32)

    @pl.when(i > 0)
    def _():
        xbuf[0:8, :] = xbuf[ts:ts + 8, :]

    x = dn_ref[0]
    xbuf[8:ts + 8, :] = x[:, :DN_QKV_DIM].astype(F32)
    y = convw_ref[0:1, :] * xbuf[pl.ds(8 - (DN_CONV - 1), ts), :]
    for j in range(1, DN_CONV):
        y = y + convw_ref[j:j + 1, :] * xbuf[pl.ds(8 - (DN_CONV - 1) + j, ts), :]
    y = _silu(y)

    gt = gate_ref[0]
    beta_all = _sigmoid(gt)
    sp = gt + dtb_ref[...]
    softplus = jnp.maximum(sp, 0.0) + jnp.log(1.0 + jnp.exp(-jnp.abs(sp)))
    g_all = -jnp.exp(alog_ref[...]) * softplus

    row = lax.broadcasted_iota(jnp.int32, (ts, ts), 0)
    col = lax.broadcasted_iota(jnp.int32, (ts, ts), 1)
    same_chunk = (row // c) == (col // c)
    causal = same_chunk & (col <= row)
    strict = same_chunk & (col < row)
    gcum = jnp.dot(causal.astype(F32), g_all, preferred_element_type=F32, precision=lax.Precision.HIGHEST)
    gtot = jnp.dot(same_chunk.astype(F32), g_all, preferred_element_type=F32, precision=lax.Precision.HIGHEST)
    gcum_t = gcum.T
    eye = (row == col).astype(F32)

    for h in range(DN_HEADS):
        q = _l2n(y[:, h * hd:(h + 1) * hd]) * (hd ** -0.5)
        k = _l2n(y[:, DN_KEY_DIM + h * hd:DN_KEY_DIM + (h + 1) * hd])
        v = y[:, 2 * DN_KEY_DIM + h * hd:2 * DN_KEY_DIM + (h + 1) * hd]
        beta = beta_all[:, h:h + 1]
        gcol = gcum[:, DN_HEADS + h:DN_HEADS + h + 1]
        grow = gcum_t[DN_HEADS + h:DN_HEADS + h + 1, :]
        gl = gtot[:, DN_HEADS + h:DN_HEADS + h + 1]
        decay = jnp.exp(jnp.where(causal, gcol - grow, NEG_BIG))
        kb = k * beta
        k16 = k.astype(BF16)
        kk = _dot_nt(kb.astype(BF16), k16)
        attn = _dot_nt(q.astype(BF16), k16) * decay
        low = jnp.where(strict, kk, 0.0) * decay
        inv = eye - low
        pw = _dot(low.astype(BF16), low.astype(BF16))
        for it in range(5):
            inv = inv + _dot(inv.astype(BF16), pw.astype(BF16))
            if it < 4:
                pw = _dot(pw.astype(BF16), pw.astype(BF16))
        egc = jnp.exp(gcol)
        rhs = jnp.concatenate([v * beta, kb * egc], axis=-1).astype(BF16)
        sol = _dot(inv.astype(BF16), rhs)
        u = sol[:, :hd]
        w = sol[:, hd:]
        q_dec = (q * egc).astype(BF16)
        k_dec = (k * jnp.exp(gl - gcol)).astype(BF16)
        w16 = w.astype(BF16)
        a16 = attn.astype(BF16)
        st = state[h]
        outs = []
        for n in range(nchunk):
            r0, r1 = n * c, (n + 1) * c
            st16 = st.astype(BF16)
            v_new = u[r0:r1] - _dot(w16[r0:r1], st16)
            vn16 = v_new.astype(BF16)
            pads = [jnp.zeros((r0, hd), BF16)] * (r0 > 0) + [vn16] + [jnp.zeros((ts - r1, hd), BF16)] * (r1 < ts)
            outs.append(_dot(q_dec[r0:r1], st16) + _dot(a16[r0:r1], jnp.concatenate(pads, axis=0)))
            st = st * jnp.exp(gl[r0:r0 + 1, :]) + _dot_tn(k_dec[r0:r1], vn16)
        state[h] = st
        o = jnp.concatenate(outs, axis=0)
        o = o * lax.rsqrt(jnp.mean(o * o, axis=-1, keepdims=True) + RMS_EPS) * ng_ref[...]
        z = x[:, DN_QKV_DIM + h * hd:DN_QKV_DIM + (h + 1) * hd].astype(F32)
        o_ref[0, :, h * hd:(h + 1) * hd] = (o * _silu(z)).astype(o_ref.dtype)


def _deltanet(dn, gates, conv_w, alog_row, dtb_row, norm_g):
    bsz, s, _ = dn.shape
    ts = min(DN_TILE, s)
    tok = lambda w: pl.BlockSpec((1, ts, w), lambda bi, i: (bi, i, 0))
    return pl.pallas_call(
        functools.partial(_deltanet_body, ts=ts),
        grid=(bsz, s // ts),
        in_specs=[tok(_DNZ), tok(GATE_PAD), _const_spec((DN_CONV, DN_QKV_DIM)),
                  _const_spec((1, GATE_PAD)), _const_spec((1, GATE_PAD)), _const_spec((1, DN_HEAD_DIM))],
        out_specs=tok(DN_KEY_DIM),
        out_shape=jax.ShapeDtypeStruct((bsz, s, DN_KEY_DIM), BF16),
        scratch_shapes=[pltpu.VMEM((ts + 8, DN_QKV_DIM), F32),
                        pltpu.VMEM((DN_HEADS, DN_HEAD_DIM, DN_HEAD_DIM), F32)],
        compiler_params=_cparams("parallel", "arbitrary"),
        name="gated_deltanet",
    )(dn, gates, conv_w, alog_row, dtb_row, norm_g)


def _dilated_body(q_ref, k_ref, v_ref, o_ref, *bufs, s):
    blk = SW_BLOCK
    obufs, lbufs = bufs[:3], bufs[3:]
    lane = lax.broadcasted_iota(jnp.int32, (blk, LANES), 1)
    head0 = lane < SW_HEAD_DIM
    qi = lax.broadcasted_iota(jnp.int32, (blk, 2 * blk), 0)
    kj = lax.broadcasted_iota(jnp.int32, (blk, 2 * blk), 1)
    dist = blk + qi - kj
    scale = SW_HEAD_DIM ** -0.5

    for br, (window, r) in enumerate(SW_BRANCHES):
        steps = window // r
        band = (dist >= 0) & (dist <= steps)
        n_iter = s // blk
        shift = r.bit_length() - 1

        def rows(start, r=r):
            if r == 1:
                return pl.ds(pl.multiple_of(start, blk), blk)
            return pl.ds(start, blk, stride=r)

        def body(idx, carry, r=r, shift=shift, band=band, br=br, rows=rows):
            rho = idx & (r - 1)
            n = idx >> shift
            start = rho + (r * blk) * n
            pstart = jnp.maximum(start - r * blk, 0)
            qb = q_ref[0, 0, rows(start), :] * scale
            kcat = jnp.concatenate([k_ref[0, 0, rows(pstart), :], k_ref[0, 0, rows(start), :]], axis=0).astype(BF16)
            vcat = jnp.concatenate([v_ref[0, 0, rows(pstart), :], v_ref[0, 0, rows(start), :]], axis=0).astype(BF16)
            valid = band & ((n > 0) | (kj >= blk))
            res = []
            for hh in range(2):
                qm = jnp.where(head0 if hh == 0 else ~head0, qb, 0.0).astype(BF16)
                sc = jnp.where(valid, _dot_nt(qm, kcat), NEG_BIG)
                m = jnp.max(sc, axis=-1, keepdims=True)
                p = jnp.exp(sc - m)
                l = jnp.sum(p, axis=-1, keepdims=True)
                o = _dot(p.astype(BF16), vcat) / l
                res.append((o, m + jnp.log(l)))
            obufs[br][rows(start), :] = jnp.where(head0, res[0][0], res[1][0])
            lbufs[br][rows(start), :] = jnp.where(head0, res[0][1], res[1][1])
            return carry

        lax.fori_loop(0, n_iter, body, 0)

    rows_per = 4 * blk

    def combine(ci, carry):
        sl = pl.ds(pl.multiple_of(ci * rows_per, rows_per), rows_per)
        l0, l1, l2 = lbufs[0][sl, :], lbufs[1][sl, :], lbufs[2][sl, :]
        mx = jnp.maximum(jnp.maximum(l0, l1), l2)
        w0, w1, w2 = jnp.exp(l0 - mx), jnp.exp(l1 - mx), jnp.exp(l2 - mx)
        out = (obufs[0][sl, :] * w0 + obufs[1][sl, :] * w1 + obufs[2][sl, :] * w2) / (w0 + w1 + w2)
        o_ref[0, sl, :] = out.astype(o_ref.dtype)
        return carry

    lax.fori_loop(0, s // rows_per, combine, 0)


def _dilated_attention(q, k, v):
    bsz, nhp, s, _ = q.shape
    spec = pl.BlockSpec((1, 1, s, LANES), lambda bi, hp: (bi, hp, 0, 0))
    return pl.pallas_call(
        functools.partial(_dilated_body, s=s),
        grid=(bsz, nhp),
        in_specs=[spec, spec, spec],
        out_specs=pl.BlockSpec((1, s, LANES), lambda bi, hp: (bi, 0, hp)),
        out_shape=jax.ShapeDtypeStruct((bsz, s, SW_DIM), BF16),
        scratch_shapes=[pltpu.VMEM((s, LANES), F32)] * 6,
        compiler_params=_cparams("parallel", "parallel"),
        name="dilated_window_attention",
    )(q, k, v)


def _hyb_out_body(h_ref, a_ref, b_ref, wa_ref, wb_ref, g_ref, bb_ref, o_ref):
    sub = _dot(a_ref[...], wa_ref[...]) + _dot(b_ref[...], wb_ref[...])
    o_ref[...] = _post_norm(h_ref[...], sub, g_ref[...], bb_ref[...])


def _hyb_out(h2, a2, b2, wa, wb, g, b):
    t, d = h2.shape
    tm = min(TM, t)
    row = lambda w: pl.BlockSpec((tm, w), lambda i: (i, 0))
    return pl.pallas_call(
        _hyb_out_body,
        grid=(t // tm,),
        in_specs=[row(d), row(DN_KEY_DIM), row(SW_DIM), _const_spec((DN_KEY_DIM, d)), _const_spec((SW_DIM, d)),
                  _const_spec((1, d)), _const_spec((1, d))],
        out_specs=row(d),
        out_shape=jax.ShapeDtypeStruct((t, d), F32),
        compiler_params=_cparams("parallel"),
        name="hybrid_out_proj_postnorm",
    )(h2, a2, b2, wa, wb, g, b)


def _transpose_body(x_ref, o_ref):
    o_ref[0] = x_ref[...].T.astype(o_ref.dtype)


def _to_chunk_major(h2c, dtype):
    nc, ld = h2c.shape
    d = D_MODEL
    l = ld // d
    tn = min(256, nc)
    return pl.pallas_call(
        _transpose_body,
        grid=(l, nc // tn),
        in_specs=[pl.BlockSpec((tn, d), lambda si, j: (j, si))],
        out_specs=pl.BlockSpec((1, d, tn), lambda si, j: (si, 0, j)),
        out_shape=jax.ShapeDtypeStruct((l, d, nc), dtype),
        compiler_params=_cparams("parallel", "parallel"),
        name="to_chunk_major",
    )(h2c)


def _untranspose_body(x_ref, o_ref):
    o_ref[...] = x_ref[0].T.astype(o_ref.dtype)


def _from_chunk_major(yq, dtype):
    l, d, nc = yq.shape
    tn = min(256, nc)
    return pl.pallas_call(
        _untranspose_body,
        grid=(l, nc // tn),
        in_specs=[pl.BlockSpec((1, d, tn), lambda si, j: (si, 0, j))],
        out_specs=pl.BlockSpec((tn, d), lambda si, j: (j, si)),
        out_shape=jax.ShapeDtypeStruct((nc, l * d), dtype),
        compiler_params=_cparams("parallel", "parallel"),
        name="from_chunk_major",
    )(yq)


def _s5_body(u_ref, toe_ref, wis_ref, wso_ref, apr_ref, api_ref, y_ref, *, ncb):
    l, gsz, nc = u_ref.shape
    p = S5_STATE
    u = u_ref[...].reshape(l * gsz, nc)
    y_intra = _dot(toe_ref[0], u)
    s_loc = _dot(wis_ref[0], u)
    apr = apr_ref[0]
    api = api_ref[0]
    lane = lax.broadcasted_iota(jnp.int32, (p, ncb), 1)
    nsteps = (ncb - 1).bit_length()
    xr_parts, xi_parts = [], []
    for bi in range(nc // ncb):
        zr = s_loc[:p, bi * ncb:(bi + 1) * ncb]
        zi = s_loc[p:, bi * ncb:(bi + 1) * ncb]
        for kk in range(nsteps):
            sh = 1 << kk
            zr_s = pltpu.roll(zr, sh, 1)
            zi_s = pltpu.roll(zi, sh, 1)
            ar = apr[:, kk:kk + 1]
            ai = api[:, kk:kk + 1]
            ok = lane >= sh
            zr, zi = (zr + jnp.where(ok, ar * zr_s - ai * zi_s, 0.0),
                      zi + jnp.where(ok, ar * zi_s + ai * zr_s, 0.0))
        xr_parts.append(jnp.where(lane >= 1, pltpu.roll(zr, 1, 1), 0.0))
        xi_parts.append(jnp.where(lane >= 1, pltpu.roll(zi, 1, 1), 0.0))
    x_in = jnp.concatenate([jnp.concatenate(xr_parts, axis=1), jnp.concatenate(xi_parts, axis=1)], axis=0)
    y = y_intra + _dot(wso_ref[0], x_in.astype(BF16))
    y_ref[...] = y.reshape(l, gsz, nc).astype(y_ref.dtype)


def _s5_core(uq, toe, wis, wso, apr, api, ncb):
    l, d, nc = uq.shape
    g = d // S5_GROUP
    lg = l * S5_GROUP
    p2 = 2 * S5_STATE
    nk = apr.shape[-1]
    per_g = lambda a, b: pl.BlockSpec((1, a, b), lambda gi: (gi, 0, 0))
    return pl.pallas_call(
        functools.partial(_s5_body, ncb=ncb),
        grid=(g,),
        in_specs=[pl.BlockSpec((l, S5_GROUP, nc), lambda gi: (0, gi, 0)),
                  per_g(lg, lg), per_g(p2, lg), per_g(lg, p2), per_g(S5_STATE, nk), per_g(S5_STATE, nk)],
        out_specs=pl.BlockSpec((l, S5_GROUP, nc), lambda gi: (0, gi, 0)),
        out_shape=jax.ShapeDtypeStruct((l, d, nc), F32),
        compiler_params=_cparams("parallel"),
        name="s5_chunked_conv",
    )(uq, toe, wis, wso, apr, api)


def _s5_tables(a_re, a_im, log_dt, b_re, b_im, c_re, c_im, ncb):
    l = S5_CHUNK
    a = lax.complex(a_re.astype(F32), a_im.astype(F32))
    dt = jnp.exp(log_dt.astype(F32))[:, None]
    adt = a * dt
    a_bar = jnp.exp(adt)
    b_bar = ((a_bar - 1.0) / a)[..., None] * lax.complex(b_re.astype(F32), b_im.astype(F32))
    cc = lax.complex(c_re.astype(F32), c_im.astype(F32))
    j = jnp.arange(l + 1, dtype=F32)
    apow = jnp.exp(adt[None] * j[:, None, None])
    kern = jnp.einsum('ghp,jgp,gpi->gjhi', cc, apow[:l], b_bar).real
    t_idx = jnp.arange(l)
    lag = t_idx[:, None] - t_idx[None, :]
    kt = kern[:, jnp.clip(lag, 0, l - 1)]
    kt = jnp.where((lag >= 0)[None, :, :, None, None], kt, 0.0)
    g = a.shape[0]
    toe = kt.transpose(0, 1, 3, 2, 4).reshape(g, l * S5_GROUP, l * S5_GROUP)
    w_in = apow[:l][::-1].transpose(1, 2, 0)[..., None] * b_bar[:, :, None, :]
    w_in = w_in.reshape(g, S5_STATE, l * S5_GROUP)
    wis = jnp.concatenate([w_in.real, w_in.imag], axis=1)
    w_out = cc[:, None, :, :] * apow[1:l + 1].transpose(1, 0, 2)[:, :, None, :]
    w_out = w_out.reshape(g, l * S5_GROUP, S5_STATE)
    wso = jnp.concatenate([w_out.real, -w_out.imag], axis=2)
    nsteps = (ncb - 1).bit_length()
    kexp = (l * (2.0 ** jnp.arange(nsteps, dtype=F32)))
    ak = jnp.exp(adt[..., None] * kexp)
    return toe.astype(BF16), wis.astype(BF16), wso.astype(BF16), ak.real, ak.imag


def _glu_body(h_ref, y_ref, d_ref, wo_ref, wg_ref, g_ref, b_ref, o_ref):
    x = h_ref[...]
    y = y_ref[...].astype(F32) + d_ref[...] * x
    hid = (y * (0.5 * (1.0 + jnp.tanh(math.sqrt(2.0 / math.pi) * (y + 0.044715 * (y * y * y)))))).astype(BF16)
    sub = _dot(hid, wo_ref[...]) * _sigmoid(_dot(hid, wg_ref[...]))
    o_ref[...] = _post_norm(x, sub, g_ref[...], b_ref[...])


def _s5_glu(h2, y2, dskip, wo, wg, g, b):
    t, d = h2.shape
    tm = min(TM, t)
    row = pl.BlockSpec((tm, d), lambda i: (i, 0))
    return pl.pallas_call(
        _glu_body,
        grid=(t // tm,),
        in_specs=[row, row, _const_spec((1, d)), _const_spec((d, d)), _const_spec((d, d)),
                  _const_spec((1, d)), _const_spec((1, d))],
        out_specs=row,
        out_shape=jax.ShapeDtypeStruct((t, d), F32),
        compiler_params=_cparams("parallel"),
        name="s5_glu_postnorm",
    )(h2, y2, dskip, wo, wg, g, b)


def _even_mixer(h, cos, sin, w_in, conv_w, a_log, dt_bias, norm_g, w_out, ln_g, ln_b):
    bsz, s, d = h.shape
    w_gate = jnp.pad(w_in[:, _DNZ:_DNZ + 2 * DN_HEADS], ((0, 0), (0, GATE_PAD - 2 * DN_HEADS)))
    w_packed = jnp.concatenate([w_in[:, :_DNZ], w_gate, w_in[:, _DNZ + 2 * DN_HEADS:]], axis=1).astype(BF16)
    dn, gates, q, k, v = _hyb_in(h, w_packed, cos, sin)
    lane_pad = lambda p: jnp.pad(p.astype(F32), (DN_HEADS, GATE_PAD - 2 * DN_HEADS))[None, :]
    a_out = _deltanet(dn, gates, conv_w.astype(F32), lane_pad(a_log), lane_pad(dt_bias),
                      norm_g.astype(F32)[None, :])
    b_out = _dilated_attention(q, k, v)
    w16 = w_out.astype(BF16)
    out = _hyb_out(h.reshape(bsz * s, d), a_out.reshape(bsz * s, DN_KEY_DIM), b_out.reshape(bsz * s, SW_DIM),
                   w16[:DN_KEY_DIM], w16[DN_KEY_DIM:], ln_g[None, :], ln_b[None, :])
    return out.reshape(bsz, s, d)


def _odd_mixer(h, a_re, a_im, log_dt, b_re, b_im, c_re, c_im, d_skip, w_o, w_g, ln_g, ln_b):
    bsz, s, d = h.shape
    l = S5_CHUNK
    ncb = s // l
    nc = bsz * ncb
    toe, wis, wso, apr, api = _s5_tables(a_re, a_im, log_dt, b_re, b_im, c_re, c_im, ncb)
    uq = _to_chunk_major(h.reshape(nc, l * d), BF16)
    yq = _s5_core(uq, toe, wis, wso, apr, api, ncb)
    y2 = _from_chunk_major(yq, F32).reshape(bsz * s, d)
    out = _s5_glu(h.reshape(bsz * s, d), y2, d_skip.astype(F32)[None, :], w_o.astype(BF16), w_g.astype(BF16),
                  ln_g[None, :], ln_b[None, :])
    return out.reshape(bsz, s, d)


def kernel(x, mem, positions, hyb_w_in, dn_conv_w, dn_a_log, dn_dt_bias, dn_norm_g, hyb_w_out, s5_a_re, s5_a_im, s5_log_dt, s5_b_re, s5_b_im, s5_c_re, s5_c_im, s5_d, s5_glu_wo, s5_glu_wg, ln_mix_g, ln_mix_b, xq_w, xk_w, xv_w, xo_w, ln_x_g, ln_x_b, ffn_wg, ffn_wu, ffn_wd, ln_ffn_g, ln_ffn_b):
    bsz, s, d = x.shape
    m = mem.shape[1]
    depth = xq_w.shape[0]
    cos, sin = _rope_tables(positions)
    wkv_all = jnp.concatenate([jnp.concatenate([xk_w[l], xv_w[l]], axis=1) for l in range(depth)],
                              axis=1).astype(BF16)
    kv_all = _mem_kv(mem.reshape(bsz * m, d), wkv_all).reshape(bsz, m, depth * 2 * d)
    h = x
    for layer in range(depth):
        i = layer // 2
        if layer % 2 == 0:
            h = _even_mixer(h, cos, sin, hyb_w_in[i], dn_conv_w[i], dn_a_log[i], dn_dt_bias[i], dn_norm_g[i],
                            hyb_w_out[i], ln_mix_g[layer], ln_mix_b[layer])
        else:
            h = _odd_mixer(h, s5_a_re[i], s5_a_im[i], s5_log_dt[i], s5_b_re[i], s5_b_im[i], s5_c_re[i],
                           s5_c_im[i], s5_d[i], s5_glu_wo[i], s5_glu_wg[i], ln_mix_g[layer], ln_mix_b[layer])
        h = _xattn(h, kv_all, layer, xq_w[layer].astype(BF16), xo_w[layer].astype(BF16),
                   ln_x_g[layer][None, :], ln_x_b[layer][None, :])
        h = _ffn(h.reshape(bsz * s, d), ffn_wg[layer].astype(BF16), ffn_wu[layer].astype(BF16),
                 ffn_wd[layer].astype(BF16), ln_ffn_g[layer][None, :], ln_ffn_b[layer][None, :]).reshape(bsz, s, d)
    return h
```

```python
import functools
import math

import jax
import jax.numpy as jnp
from jax import lax
from jax.experimental import pallas as pl
from jax.experimental.pallas import tpu as pltpu

F32 = jnp.float32
BF16 = jnp.bfloat16

D_MODEL = 1024
DEPTH = 4
DN_HEADS = 4
DN_HEAD_DIM = 128
DN_KEY_DIM = DN_HEADS * DN_HEAD_DIM
DN_QKV_DIM = 3 * DN_KEY_DIM
DN_CONV = 4
DN_CHUNK = 64
SW_HEADS = 8
SW_HEAD_DIM = 64
SW_DIM = SW_HEADS * SW_HEAD_DIM
SW_BRANCHES = ((128, 1), (512, 4), (2048, 16))
SW_BLOCK = 128
ROPE_THETA = 10000.0
S5_GROUP = 16
S5_GROUPS = D_MODEL // S5_GROUP
S5_STATE = 64
S5_CHUNK = 32
X_HEADS = 4
X_HEAD_DIM = D_MODEL // X_HEADS
DEEPNORM_ALPHA = (2 * DEPTH) ** 0.25
LN_EPS = 1e-5
RMS_EPS = 1e-6

LANES = 128
GATE_PAD = LANES
VMEM_LIMIT = 56 * 1024 * 1024
NEG_BIG = -1e30

TM = 512
DN_TILE = 256
DIL_UNROLL = 4


def _cparams(*sem):
    return pltpu.CompilerParams(dimension_semantics=sem, vmem_limit_bytes=VMEM_LIMIT)


def _const_spec(shape):
    nd = len(shape)
    return pl.BlockSpec(shape, lambda *_: (0,) * nd, pipeline_mode=pl.Buffered(1))


def _dot(a, b):
    return jnp.dot(a, b, preferred_element_type=F32)


def _dot_nt(a, b):
    return lax.dot_general(a, b, (((1,), (1,)), ((), ())), preferred_element_type=F32)


def _dot_tn(a, b):
    return lax.dot_general(a, b, (((0,), (0,)), ((), ())), preferred_element_type=F32)


def _post_norm(h, sub, g, b):
    y = DEEPNORM_ALPHA * h + sub
    mu = jnp.mean(y, axis=-1, keepdims=True)
    yc = y - mu
    var = jnp.mean(yc * yc, axis=-1, keepdims=True)
    return yc * lax.rsqrt(var + LN_EPS) * g + b


def _sigmoid(x):
    return 1.0 / (1.0 + jnp.exp(-x))


def _silu(x):
    return x * _sigmoid(x)


def _ffn_body(h_ref, wg_ref, wu_ref, wd_ref, g_ref, b_ref, o_ref, *, n_chunks, fc):
    x = h_ref[...]
    xb = x.astype(BF16)
    acc = jnp.zeros(x.shape, F32)
    for c in range(n_chunks):
        lo, hi = c * fc, (c + 1) * fc
        gate = _dot(xb, wg_ref[:, lo:hi])
        up = _dot(xb, wu_ref[:, lo:hi])
        act = (_silu(gate) * up).astype(BF16)
        acc = acc + _dot(act, wd_ref[lo:hi, :])
    o_ref[...] = _post_norm(x, acc, g_ref[...], b_ref[...])


def _ffn(h2, wg, wu, wd, g, b):
    t, d = h2.shape
    f = wg.shape[1]
    n_chunks = 2
    fc = f // n_chunks
    tm = min(TM, t)
    return pl.pallas_call(
        functools.partial(_ffn_body, n_chunks=n_chunks, fc=fc),
        grid=(t // tm,),
        in_specs=[pl.BlockSpec((tm, d), lambda i: (i, 0)),
                  _const_spec((d, f)), _const_spec((d, f)), _const_spec((f, d)),
                  _const_spec((1, d)), _const_spec((1, d))],
        out_specs=pl.BlockSpec((tm, d), lambda i: (i, 0)),
        out_shape=jax.ShapeDtypeStruct((t, d), F32),
        compiler_params=_cparams("parallel"),
        name="swiglu_postnorm",
    )(h2, wg, wu, wd, g, b)


def _mm_body(x_ref, w_ref, o_ref):
    o_ref[...] = _dot(x_ref[...].astype(BF16), w_ref[...]).astype(o_ref.dtype)


def _mem_kv(mem2, wkv_all):
    r, d = mem2.shape
    n = wkv_all.shape[1]
    nb = 2 * d
    tm = min(TM, r)
    return pl.pallas_call(
        _mm_body,
        grid=(n // nb, r // tm),
        in_specs=[pl.BlockSpec((tm, d), lambda j, i: (i, 0)),
                  pl.BlockSpec((d, nb), lambda j, i: (0, j))],
        out_specs=pl.BlockSpec((tm, nb), lambda j, i: (i, j)),
        out_shape=jax.ShapeDtypeStruct((r, n), BF16),
        compiler_params=_cparams("parallel", "parallel"),
        name="memory_kv_proj",
    )(mem2, wkv_all)


def _xattn_body(h_ref, kv_ref, wq_ref, wo_ref, g_ref, b_ref, o_ref):
    x = h_ref[0]
    d = x.shape[-1]
    q = _dot(x.astype(BF16), wq_ref[...]) * (X_HEAD_DIM ** -0.5)
    qb = q.astype(BF16)
    outs = []
    for hh in range(X_HEADS):
        lo, hi = hh * X_HEAD_DIM, (hh + 1) * X_HEAD_DIM
        k = kv_ref[0, :, lo:hi]
        v = kv_ref[0, :, d + lo:d + hi]
        s = _dot_nt(qb[:, lo:hi], k)
        m = jnp.max(s, axis=-1, keepdims=True)
        p = jnp.exp(s - m)
        l = jnp.sum(p, axis=-1, keepdims=True)
        outs.append(_dot(p.astype(BF16), v) / l)
    o = jnp.concatenate(outs, axis=-1).astype(BF16)
    o_ref[0] = _post_norm(x, _dot(o, wo_ref[...]), g_ref[...], b_ref[...])


def _xattn(h, kv_all, layer, wq, wo, g, b):
    bsz, s, d = h.shape
    m = kv_all.shape[1]
    tm = min(TM, s)
    return pl.pallas_call(
        _xattn_body,
        grid=(bsz, s // tm),
        in_specs=[pl.BlockSpec((1, tm, d), lambda bi, i: (bi, i, 0)),
                  pl.BlockSpec((1, m, 2 * d), lambda bi, i: (bi, 0, layer)),
                  _const_spec((d, d)), _const_spec((d, d)),
                  _const_spec((1, d)), _const_spec((1, d))],
        out_specs=pl.BlockSpec((1, tm, d), lambda bi, i: (bi, i, 0)),
        out_shape=jax.ShapeDtypeStruct((bsz, s, d), F32),
        compiler_params=_cparams("parallel", "parallel"),
        name="memory_xattn_postnorm",
    )(h, kv_all, wq, wo, g, b)


def _rope_table_body(pos_ref, invf_ref, sgn_ref, cos_ref, sin_ref):
    ang = pos_ref[0] * invf_ref[...]
    cos_ref[0] = jnp.cos(ang)
    sin_ref[0] = jnp.sin(ang) * sgn_ref[...]


def _rope_tables(positions):
    bsz, s = positions.shape
    half = SW_HEAD_DIM // 2
    inv_freq = ROPE_THETA ** (-jnp.arange(0, SW_HEAD_DIM, 2, dtype=F32) / SW_HEAD_DIM)
    invf = jnp.tile(inv_freq, LANES // half)[None, :]
    sgn = jnp.tile(jnp.concatenate([-jnp.ones((half,), F32), jnp.ones((half,), F32)]),
                   LANES // SW_HEAD_DIM)[None, :]
    posf = positions.astype(F32)[..., None]
    tm = min(TM, s)
    return pl.pallas_call(
        _rope_table_body,
        grid=(bsz, s // tm),
        in_specs=[pl.BlockSpec((1, tm, 1), lambda bi, i: (bi, i, 0)),
                  _const_spec((1, LANES)), _const_spec((1, LANES))],
        out_specs=[pl.BlockSpec((1, tm, LANES), lambda bi, i: (bi, i, 0))] * 2,
        out_shape=[jax.ShapeDtypeStruct((bsz, s, LANES), F32)] * 2,
        compiler_params=_cparams("parallel", "parallel"),
        name="rope_tables",
    )(posf, invf, sgn)


_DNZ = DN_QKV_DIM + DN_KEY_DIM
_SW_OFF = _DNZ + GATE_PAD
_HYB_PACKED = _SW_OFF + 3 * SW_DIM


def _hyb_in_body(h_ref, w_ref, cos_ref, sin_ref, dn_ref, gate_ref, q_ref, k_ref, v_ref):
    x = h_ref[0].astype(BF16)
    dn_ref[0] = _dot(x, w_ref[:, :_DNZ]).astype(dn_ref.dtype)
    gate_ref[0] = _dot(x, w_ref[:, _DNZ:_SW_OFF])
    cos = cos_ref[0]
    sin = sin_ref[0]
    lane = lax.broadcasted_iota(jnp.int32, cos.shape, 1)
    first_half = (lane % SW_HEAD_DIM) < (SW_HEAD_DIM // 2)
    half = SW_HEAD_DIM // 2
    for idx, out_ref in enumerate((q_ref, k_ref, v_ref)):
        lo = _SW_OFF + idx * SW_DIM
        y = _dot(x, w_ref[:, lo:lo + SW_DIM])
        for hp in range(SW_DIM // LANES):
            yy = y[:, hp * LANES:(hp + 1) * LANES]
            if idx < 2:
                swapped = jnp.where(first_half, pltpu.roll(yy, LANES - half, 1), pltpu.roll(yy, half, 1))
                yy = yy * cos + swapped * sin
            out_ref[0, hp] = yy


def _hyb_in(h, w_packed, cos, sin):
    bsz, s, d = h.shape
    tm = min(TM, s)
    nhp = SW_DIM // LANES
    qkv_shape = jax.ShapeDtypeStruct((bsz, nhp, s, LANES), F32)
    qkv_spec = pl.BlockSpec((1, nhp, tm, LANES), lambda bi, i: (bi, 0, i, 0))
    tok = lambda w: pl.BlockSpec((1, tm, w), lambda bi, i: (bi, i, 0))
    return pl.pallas_call(
        _hyb_in_body,
        grid=(bsz, s // tm),
        in_specs=[tok(d), _const_spec((d, _HYB_PACKED)), tok(LANES), tok(LANES)],
        out_specs=[tok(_DNZ), tok(GATE_PAD), qkv_spec, qkv_spec, qkv_spec],
        out_shape=[jax.ShapeDtypeStruct((bsz, s, _DNZ), BF16),
                   jax.ShapeDtypeStruct((bsz, s, GATE_PAD), F32),
                   qkv_shape, qkv_shape, qkv_shape],
        compiler_params=_cparams("parallel", "parallel"),
        name="hybrid_in_proj_rope",
    )(h, w_packed, cos, sin)


def _l2n(x):
    return x * lax.rsqrt(jnp.sum(x * x, axis=-1, keepdims=True) + RMS_EPS)


def _deltanet_body(dn_ref, gate_ref, convw_ref, alog_ref, dtb_ref, ng_ref, o_ref, xbuf, state, *, ts):
    i = pl.program_id(1)
    c = DN_CHUNK
    nchunk = ts // c
    hd = DN_HEAD_DIM

    @pl.when(i == 0)
    def _():
        xbuf[0:8, :] = jnp.zeros((8, DN_QKV_DIM), F32)
        state[...] = jnp.zeros(state.shape, F32)

    @pl.when(i > 0)
    def _():
        xbuf[0:8, :] = xbuf[ts:ts + 8, :]

    x = dn_ref[0]
    xbuf[8:ts + 8, :] = x[:, :DN_QKV_DIM].astype(F32)
    y = convw_ref[0:1, :] * xbuf[pl.ds(8 - (DN_CONV - 1), ts), :]
    for j in range(1, DN_CONV):
        y = y + convw_ref[j:j + 1, :] * xbuf[pl.ds(8 - (DN_CONV - 1) + j, ts), :]
    y = _silu(y)

    gt = gate_ref[0]
    beta_all = _sigmoid(gt)
    sp = gt + dtb_ref[...]
    softplus = jnp.maximum(sp, 0.0) + jnp.log(1.0 + jnp.exp(-jnp.abs(sp)))
    g_all = -jnp.exp(alog_ref[...]) * softplus

    row = lax.broadcasted_iota(jnp.int32, (ts, ts), 0)
    col = lax.broadcasted_iota(jnp.int32, (ts, ts), 1)
    same_chunk = (row // c) == (col // c)
    causal = same_chunk & (col <= row)
    strict = same_chunk & (col < row)
    gcum = jnp.dot(causal.astype(F32), g_all, preferred_element_type=F32, precision=lax.Precision.HIGHEST)
    gtot = jnp.dot(same_chunk.astype(F32), g_all, preferred_element_type=F32, precision=lax.Precision.HIGHEST)
    gcum_t = gcum.T
    eye = (row == col).astype(F32)

    heads = range(DN_HEADS)
    pre = []
    for h in heads:
        q = _l2n(y[:, h * hd:(h + 1) * hd]) * (hd ** -0.5)
        k = _l2n(y[:, DN_KEY_DIM + h * hd:DN_KEY_DIM + (h + 1) * hd])
        v = y[:, 2 * DN_KEY_DIM + h * hd:2 * DN_KEY_DIM + (h + 1) * hd]
        beta = beta_all[:, h:h + 1]
        gcol = gcum[:, DN_HEADS + h:DN_HEADS + h + 1]
        grow = gcum_t[DN_HEADS + h:DN_HEADS + h + 1, :]
        gl = gtot[:, DN_HEADS + h:DN_HEADS + h + 1]
        decay = jnp.exp(jnp.where(causal, gcol - grow, NEG_BIG))
        kb = k * beta
        k16 = k.astype(BF16)
        kk = _dot_nt(kb.astype(BF16), k16)
        a16 = (_dot_nt(q.astype(BF16), k16) * decay).astype(BF16)
        low = jnp.where(strict, kk, 0.0) * decay
        egc = jnp.exp(gcol)
        rhs = jnp.concatenate([v * beta, kb * egc], axis=-1).astype(BF16)
        q_dec = (q * egc).astype(BF16)
        k_dec = (k * jnp.exp(gl - gcol)).astype(BF16)
        pre.append((low, rhs, q_dec, k_dec, a16, gl))

    invs = [eye - p[0] for p in pre]
    pws = [_dot(p[0].astype(BF16), p[0].astype(BF16)) for p in pre]
    for it in range(5):
        pw16 = [pw.astype(BF16) for pw in pws]
        invs = [inv + _dot(inv.astype(BF16), p16) for inv, p16 in zip(invs, pw16)]
        if it < 4:
            pws = [_dot(p16, p16) for p16 in pw16]
    sols = [_dot(inv.astype(BF16), p[1]) for inv, p in zip(invs, pre)]
    us = [sol[:, :hd] for sol in sols]
    w16s = [sol[:, hd:].astype(BF16) for sol in sols]

    sts = [state[h] for h in heads]
    outs = [[] for _ in heads]
    for n in range(nchunk):
        r0, r1 = n * c, (n + 1) * c
        for h in heads:
            _, _, q_dec, k_dec, a16, gl = pre[h]
            st16 = sts[h].astype(BF16)
            v_new = us[h][r0:r1] - _dot(w16s[h][r0:r1], st16)
            vn16 = v_new.astype(BF16)
            pads = [jnp.zeros((r0, hd), BF16)] * (r0 > 0) + [vn16] + [jnp.zeros((ts - r1, hd), BF16)] * (r1 < ts)
            outs[h].append(_dot(q_dec[r0:r1], st16) + _dot(a16[r0:r1], jnp.concatenate(pads, axis=0)))
            sts[h] = sts[h] * jnp.exp(gl[r0:r0 + 1, :]) + _dot_tn(k_dec[r0:r1], vn16)
    for h in heads:
        state[h] = sts[h]
        o = jnp.concatenate(outs[h], axis=0)
        o = o * lax.rsqrt(jnp.mean(o * o, axis=-1, keepdims=True) + RMS_EPS) * ng_ref[...]
        z = x[:, DN_QKV_DIM + h * hd:DN_QKV_DIM + (h + 1) * hd].astype(F32)
        o_ref[0, :, h * hd:(h + 1) * hd] = (o * _silu(z)).astype(o_ref.dtype)


def _deltanet(dn, gates, conv_w, alog_row, dtb_row, norm_g):
    bsz, s, _ = dn.shape
    ts = min(DN_TILE, s)
    tok = lambda w: pl.BlockSpec((1, ts, w), lambda bi, i: (bi, i, 0))
    return pl.pallas_call(
        functools.partial(_deltanet_body, ts=ts),
        grid=(bsz, s // ts),
        in_specs=[tok(_DNZ), tok(GATE_PAD), _const_spec((DN_CONV, DN_QKV_DIM)),
                  _const_spec((1, GATE_PAD)), _const_spec((1, GATE_PAD)), _const_spec((1, DN_HEAD_DIM))],
        out_specs=tok(DN_KEY_DIM),
        out_shape=jax.ShapeDtypeStruct((bsz, s, DN_KEY_DIM), BF16),
        scratch_shapes=[pltpu.VMEM((ts + 8, DN_QKV_DIM), F32),
                        pltpu.VMEM((DN_HEADS, DN_HEAD_DIM, DN_HEAD_DIM), F32)],
        compiler_params=_cparams("parallel", "arbitrary"),
        name="gated_deltanet",
    )(dn, gates, conv_w, alog_row, dtb_row, norm_g)


def _dilated_body(q_ref, k_ref, v_ref, o_ref, *bufs, s):
    blk = SW_BLOCK
    obufs, lbufs = bufs[:3], bufs[3:]
    lane = lax.broadcasted_iota(jnp.int32, (blk, LANES), 1)
    head0 = lane < SW_HEAD_DIM
    qi = lax.broadcasted_iota(jnp.int32, (2 * blk, 2 * blk), 0) % blk
    kj = lax.broadcasted_iota(jnp.int32, (2 * blk, 2 * blk), 1)
    dist = blk + qi - kj
    scale = SW_HEAD_DIM ** -0.5

    for br, (window, r) in enumerate(SW_BRANCHES):
        steps = window // r
        band = (dist >= 0) & (dist <= steps)
        nblk = s // (r * blk)
        n_count = min(DIL_UNROLL, nblk)
        rho_count = DIL_UNROLL // n_count
        nrg = r // rho_count

        def rows(start, r=r):
            if r == 1:
                return pl.ds(pl.multiple_of(start, blk), blk)
            return pl.ds(start, blk, stride=r)

        def body(it, carry, r=r, band=band, br=br, rows=rows, n_count=n_count, rho_count=rho_count, nrg=nrg):
            rho_base = (it & (nrg - 1)) * rho_count
            n0 = (it >> (nrg.bit_length() - 1)) * n_count
            starts, kcats, vcats, firsts = [], [], [], []
            for dr in range(rho_count):
                base = rho_base + dr + (r * blk) * n0
                pstart = jnp.maximum(base - r * blk, 0)
                kprev = k_ref[0, 0, rows(pstart), :].astype(BF16)
                vprev = v_ref[0, 0, rows(pstart), :].astype(BF16)
                for dn in range(n_count):
                    start = base + (r * blk) * dn
                    kcur = k_ref[0, 0, rows(start), :].astype(BF16)
                    vcur = v_ref[0, 0, rows(start), :].astype(BF16)
                    starts.append(start)
                    kcats.append(jnp.concatenate([kprev, kcur], axis=0))
                    vcats.append(jnp.concatenate([vprev, vcur], axis=0))
                    firsts.append(dn == 0)
                    kprev, vprev = kcur, vcur
            scs = []
            for start, kcat in zip(starts, kcats):
                qb = q_ref[0, 0, rows(start), :] * scale
                q2 = jnp.concatenate([jnp.where(head0, qb, 0.0), jnp.where(head0, 0.0, qb)], axis=0)
                scs.append(_dot_nt(q2.astype(BF16), kcat))
            first_valid = band & ((n0 > 0) | (kj >= blk))
            scs = [jnp.where(first_valid if first else band, sc, NEG_BIG) for sc, first in zip(scs, firsts)]
            ms = [jnp.max(sc, axis=-1, keepdims=True) for sc in scs]
            ps = [jnp.exp(sc - m) for sc, m in zip(scs, ms)]
            ls = [jnp.sum(p, axis=-1, keepdims=True) for p in ps]
            o2s = [_dot(p.astype(BF16), vcat) * (1.0 / l) for p, vcat, l in zip(ps, vcats, ls)]
            for start, o2, m, l in zip(starts, o2s, ms, ls):
                lse2 = m + jnp.log(l)
                obufs[br][rows(start), :] = jnp.where(head0, o2[:blk], o2[blk:])
                lbufs[br][rows(start), :] = jnp.where(head0, lse2[:blk], lse2[blk:])
            return carry

        lax.fori_loop(0, nrg * (nblk // n_count), body, 0)

    rows_per = 4 * blk

    def combine(ci, carry):
        sl = pl.ds(pl.multiple_of(ci * rows_per, rows_per), rows_per)
        l0, l1, l2 = lbufs[0][sl, :], lbufs[1][sl, :], lbufs[2][sl, :]
        mx = jnp.maximum(jnp.maximum(l0, l1), l2)
        w0, w1, w2 = jnp.exp(l0 - mx), jnp.exp(l1 - mx), jnp.exp(l2 - mx)
        out = (obufs[0][sl, :] * w0 + obufs[1][sl, :] * w1 + obufs[2][sl, :] * w2) / (w0 + w1 + w2)
        o_ref[0, sl, :] = out.astype(o_ref.dtype)
        return carry

    lax.fori_loop(0, s // rows_per, combine, 0)


def _dilated_attention(q, k, v):
    bsz, nhp, s, _ = q.shape
    spec = pl.BlockSpec((1, 1, s, LANES), lambda bi, hp: (bi, hp, 0, 0))
    return pl.pallas_call(
        functools.partial(_dilated_body, s=s),
        grid=(bsz, nhp),
        in_specs=[spec, spec, spec],
        out_specs=pl.BlockSpec((1, s, LANES), lambda bi, hp: (bi, 0, hp)),
        out_shape=jax.ShapeDtypeStruct((bsz, s, SW_DIM), BF16),
        scratch_shapes=[pltpu.VMEM((s, LANES), F32)] * 6,
        compiler_params=_cparams("parallel", "parallel"),
        name="dilated_window_attention",
    )(q, k, v)


def _hyb_out_body(h_ref, a_ref, b_ref, wa_ref, wb_ref, g_ref, bb_ref, o_ref):
    sub = _dot(a_ref[...], wa_ref[...]) + _dot(b_ref[...], wb_ref[...])
    o_ref[...] = _post_norm(h_ref[...], sub, g_ref[...], bb_ref[...])


def _hyb_out(h2, a2, b2, wa, wb, g, b):
    t, d = h2.shape
    tm = min(TM, t)
    row = lambda w: pl.BlockSpec((tm, w), lambda i: (i, 0))
    return pl.pallas_call(
        _hyb_out_body,
        grid=(t // tm,),
        in_specs=[row(d), row(DN_KEY_DIM), row(SW_DIM), _const_spec((DN_KEY_DIM, d)), _const_spec((SW_DIM, d)),
                  _const_spec((1, d)), _const_spec((1, d))],
        out_specs=row(d),
        out_shape=jax.ShapeDtypeStruct((t, d), F32),
        compiler_params=_cparams("parallel"),
        name="hybrid_out_proj_postnorm",
    )(h2, a2, b2, wa, wb, g, b)


S5_PITCH = 40


def _s5_body(h_ref, toe_ref, wis_ref, wso_ref, apr_ref, api_ref, y_ref, xpad, ypad, ubuf, ybuf, *, ncb):
    l = S5_CHUNK
    gsz = S5_GROUP
    p = S5_STATE
    ngroups = LANES // gsz

    def copy_in(c, carry):
        xpad[pl.ds(pl.multiple_of(c * S5_PITCH, 8), l), :] = h_ref[0, pl.ds(pl.multiple_of(c * l, l), l), :]
        return carry

    lax.fori_loop(0, ncb, copy_in, 0, unroll=8)

    def gather(si, carry):
        ubuf[si] = xpad[pl.ds(si, ncb, stride=S5_PITCH), :].T.astype(BF16)
        return carry

    lax.fori_loop(0, l, gather, 0, unroll=4)

    lane = lax.broadcasted_iota(jnp.int32, (p, ncb), 1)
    nsteps = (ncb - 1).bit_length()
    for gi in range(ngroups):
        u = ubuf[:, gi * gsz:(gi + 1) * gsz, :].reshape(l * gsz, ncb)
        y_intra = _dot(toe_ref[gi], u)
        s_loc = _dot(wis_ref[gi], u)
        apr = apr_ref[gi]
        api = api_ref[gi]
        zr = s_loc[:p]
        zi = s_loc[p:]
        for kk in range(nsteps):
            sh = 1 << kk
            zr_s = pltpu.roll(zr, sh, 1)
            zi_s = pltpu.roll(zi, sh, 1)
            ar = apr[:, kk:kk + 1]
            ai = api[:, kk:kk + 1]
            ok = lane >= sh
            zr, zi = (zr + jnp.where(ok, ar * zr_s - ai * zi_s, 0.0),
                      zi + jnp.where(ok, ar * zi_s + ai * zr_s, 0.0))
        x_in = jnp.concatenate([jnp.where(lane >= 1, pltpu.roll(zr, 1, 1), 0.0),
                                jnp.where(lane >= 1, pltpu.roll(zi, 1, 1), 0.0)], axis=0)
        y = y_intra + _dot(wso_ref[gi], x_in.astype(BF16))
        ybuf[:, gi * gsz:(gi + 1) * gsz, :] = y.reshape(l, gsz, ncb)

    def scatter(si, carry):
        ypad[pl.ds(si, ncb, stride=S5_PITCH), :] = ybuf[si].T
        return carry

    lax.fori_loop(0, l, scatter, 0, unroll=4)

    def copy_out(c, carry):
        y_ref[0, pl.ds(pl.multiple_of(c * l, l), l), :] = (
            ypad[pl.ds(pl.multiple_of(c * S5_PITCH, 8), l), :].astype(y_ref.dtype))
        return carry

    lax.fori_loop(0, ncb, copy_out, 0, unroll=8)


def _s5_conv(h, toe, wis, wso, apr, api):
    bsz, s, d = h.shape
    l = S5_CHUNK
    ncb = s // l
    ngroups = LANES // S5_GROUP
    lg = l * S5_GROUP
    p2 = 2 * S5_STATE
    nk = apr.shape[-1]
    per_tile = lambda a, b: pl.BlockSpec((ngroups, a, b), lambda j, bi: (j, 0, 0))
    tok = pl.BlockSpec((1, s, LANES), lambda j, bi: (bi, 0, j))
    return pl.pallas_call(
        functools.partial(_s5_body, ncb=ncb),
        grid=(d // LANES, bsz),
        in_specs=[tok, per_tile(lg, lg), per_tile(p2, lg), per_tile(lg, p2),
                  per_tile(S5_STATE, nk), per_tile(S5_STATE, nk)],
        out_specs=tok,
        out_shape=jax.ShapeDtypeStruct((bsz, s, d), BF16),
        scratch_shapes=[pltpu.VMEM((ncb * S5_PITCH, LANES), F32), pltpu.VMEM((ncb * S5_PITCH, LANES), F32),
                        pltpu.VMEM((l, LANES, ncb), BF16), pltpu.VMEM((l, LANES, ncb), F32)],
        compiler_params=_cparams("parallel", "parallel"),
        name="s5_chunked_conv",
    )(h, toe, wis, wso, apr, api)


def _s5_tables(a_re, a_im, log_dt, b_re, b_im, c_re, c_im, ncb):
    l = S5_CHUNK
    a = lax.complex(a_re.astype(F32), a_im.astype(F32))
    dt = jnp.exp(log_dt.astype(F32))[:, None]
    adt = a * dt
    a_bar = jnp.exp(adt)
    b_bar = ((a_bar - 1.0) / a)[..., None] * lax.complex(b_re.astype(F32), b_im.astype(F32))
    cc = lax.complex(c_re.astype(F32), c_im.astype(F32))
    j = jnp.arange(l + 1, dtype=F32)
    apow = jnp.exp(adt[None] * j[:, None, None])
    kern = jnp.einsum('ghp,jgp,gpi->gjhi', cc, apow[:l], b_bar).real
    t_idx = jnp.arange(l)
    lag = t_idx[:, None] - t_idx[None, :]
    kt = kern[:, jnp.clip(lag, 0, l - 1)]
    kt = jnp.where((lag >= 0)[None, :, :, None, None], kt, 0.0)
    g = a.shape[0]
    toe = kt.transpose(0, 1, 3, 2, 4).reshape(g, l * S5_GROUP, l * S5_GROUP)
    w_in = apow[:l][::-1].transpose(1, 2, 0)[..., None] * b_bar[:, :, None, :]
    w_in = w_in.reshape(g, S5_STATE, l * S5_GROUP)
    wis = jnp.concatenate([w_in.real, w_in.imag], axis=1)
    w_out = cc[:, None, :, :] * apow[1:l + 1].transpose(1, 0, 2)[:, :, None, :]
    w_out = w_out.reshape(g, l * S5_GROUP, S5_STATE)
    wso = jnp.concatenate([w_out.real, -w_out.imag], axis=2)
    nsteps = (ncb - 1).bit_length()
    kexp = (l * (2.0 ** jnp.arange(nsteps, dtype=F32)))
    ak = jnp.exp(adt[..., None] * kexp)
    return toe.astype(BF16), wis.astype(BF16), wso.astype(BF16), ak.real, ak.imag


def _glu_body(h_ref, y_ref, d_ref, wo_ref, wg_ref, g_ref, b_ref, o_ref):
    x = h_ref[...]
    y = y_ref[...].astype(F32) + d_ref[...] * x
    hid = (y * (0.5 * (1.0 + jnp.tanh(math.sqrt(2.0 / math.pi) * (y + 0.044715 * (y * y * y)))))).astype(BF16)
    sub = _dot(hid, wo_ref[...]) * _sigmoid(_dot(hid, wg_ref[...]))
    o_ref[...] = _post_norm(x, sub, g_ref[...], b_ref[...])


def _s5_glu(h2, y2, dskip, wo, wg, g, b):
    t, d = h2.shape
    tm = min(TM, t)
    row = pl.BlockSpec((tm, d), lambda i: (i, 0))
    return pl.pallas_call(
        _glu_body,
        grid=(t // tm,),
        in_specs=[row, row, _const_spec((1, d)), _const_spec((d, d)), _const_spec((d, d)),
                  _const_spec((1, d)), _const_spec((1, d))],
        out_specs=row,
        out_shape=jax.ShapeDtypeStruct((t, d), F32),
        compiler_params=_cparams("parallel"),
        name="s5_glu_postnorm",
    )(h2, y2, dskip, wo, wg, g, b)


def _even_mixer(h, cos, sin, w_in, conv_w, a_log, dt_bias, norm_g, w_out, ln_g, ln_b):
    bsz, s, d = h.shape
    w_gate = jnp.pad(w_in[:, _DNZ:_DNZ + 2 * DN_HEADS], ((0, 0), (0, GATE_PAD - 2 * DN_HEADS)))
    w_packed = jnp.concatenate([w_in[:, :_DNZ], w_gate, w_in[:, _DNZ + 2 * DN_HEADS:]], axis=1).astype(BF16)
    dn, gates, q, k, v = _hyb_in(h, w_packed, cos, sin)
    lane_pad = lambda p: jnp.pad(p.astype(F32), (DN_HEADS, GATE_PAD - 2 * DN_HEADS))[None, :]
    a_out = _deltanet(dn, gates, conv_w.astype(F32), lane_pad(a_log), lane_pad(dt_bias),
                      norm_g.astype(F32)[None, :])
    b_out = _dilated_attention(q, k, v)
    w16 = w_out.astype(BF16)
    out = _hyb_out(h.reshape(bsz * s, d), a_out.reshape(bsz * s, DN_KEY_DIM), b_out.reshape(bsz * s, SW_DIM),
                   w16[:DN_KEY_DIM], w16[DN_KEY_DIM:], ln_g[None, :], ln_b[None, :])
    return out.reshape(bsz, s, d)


def _odd_mixer(h, a_re, a_im, log_dt, b_re, b_im, c_re, c_im, d_skip, w_o, w_g, ln_g, ln_b):
    bsz, s, d = h.shape
    toe, wis, wso, apr, api = _s5_tables(a_re, a_im, log_dt, b_re, b_im, c_re, c_im, s // S5_CHUNK)
    y = _s5_conv(h, toe, wis, wso, apr, api)
    out = _s5_glu(h.reshape(bsz * s, d), y.reshape(bsz * s, d), d_skip.astype(F32)[None, :], w_o.astype(BF16),
                  w_g.astype(BF16), ln_g[None, :], ln_b[None, :])
    return out.reshape(bsz, s, d)


def kernel(x, mem, positions, hyb_w_in, dn_conv_w, dn_a_log, dn_dt_bias, dn_norm_g, hyb_w_out, s5_a_re, s5_a_im, s5_log_dt, s5_b_re, s5_b_im, s5_c_re, s5_c_im, s5_d, s5_glu_wo, s5_glu_wg, ln_mix_g, ln_mix_b, xq_w, xk_w, xv_w, xo_w, ln_x_g, ln_x_b, ffn_wg, ffn_wu, ffn_wd, ln_ffn_g, ln_ffn_b):
    bsz, s, d = x.shape
    m = mem.shape[1]
    depth = xq_w.shape[0]
    cos, sin = _rope_tables(positions)
    wkv_all = jnp.concatenate([jnp.concatenate([xk_w[l], xv_w[l]], axis=1) for l in range(depth)],
                              axis=1).astype(BF16)
    kv_all = _mem_kv(mem.reshape(bsz * m, d), wkv_all).reshape(bsz, m, depth * 2 * d)
    h = x
    for layer in range(depth):
        i = layer // 2
        if layer % 2 == 0:
            h = _even_mixer(h, cos, sin, hyb_w_in[i], dn_conv_w[i], dn_a_log[i], dn_dt_bias[i], dn_norm_g[i],
                            hyb_w_out[i], ln_mix_g[layer], ln_mix_b[layer])
        else:
            h = _odd_mixer(h, s5_a_re[i], s5_a_im[i], s5_log_dt[i], s5_b_re[i], s5_b_im[i], s5_c_re[i],
                           s5_c_im[i], s5_d[i], s5_glu_wo[i], s5_glu_wg[i], ln_mix_g[layer], ln_mix_b[layer])
        h = _xattn(h, kv_all, layer, xq_w[layer].astype(BF16), xo_w[layer].astype(BF16),
                   ln_x_g[layer][None, :], ln_x_b[layer][None, :])
        h = _ffn(h.reshape(bsz * s, d), ffn_wg[layer].astype(BF16), ffn_wu[layer].astype(BF16),
                 ffn_wd[layer].astype(BF16), ln_ffn_g[layer][None, :], ln_ffn_b[layer][None, :]).reshape(bsz, s, d)
    return h
```

```python
import functools
import math

import jax
import jax.numpy as jnp
from jax import lax
from jax.experimental import pallas as pl
from jax.experimental.pallas import tpu as pltpu

F32 = jnp.float32
BF16 = jnp.bfloat16

D_MODEL = 1024
DEPTH = 4
DN_HEADS = 4
DN_HEAD_DIM = 128
DN_KEY_DIM = DN_HEADS * DN_HEAD_DIM
DN_QKV_DIM = 3 * DN_KEY_DIM
DN_CONV = 4
DN_CHUNK = 128
SW_HEADS = 8
SW_HEAD_DIM = 64
SW_DIM = SW_HEADS * SW_HEAD_DIM
SW_BRANCHES = ((128, 1), (512, 4), (2048, 16))
SW_BLOCK = 128
ROPE_THETA = 10000.0
S5_GROUP = 16
S5_GROUPS = D_MODEL // S5_GROUP
S5_STATE = 64
S5_CHUNK = 32
X_HEADS = 4
X_HEAD_DIM = D_MODEL // X_HEADS
DEEPNORM_ALPHA = (2 * DEPTH) ** 0.25
LN_EPS = 1e-5
RMS_EPS = 1e-6

LANES = 128
GATE_PAD = LANES
VMEM_LIMIT = 56 * 1024 * 1024
NEG_BIG = -1e30

TM = 512
DN_TILE = 256
DIL_UNROLL = 4


def _cparams(*sem):
    return pltpu.CompilerParams(dimension_semantics=sem, vmem_limit_bytes=VMEM_LIMIT)


def _const_spec(shape):
    nd = len(shape)
    return pl.BlockSpec(shape, lambda *_: (0,) * nd, pipeline_mode=pl.Buffered(1))


def _dot(a, b):
    return jnp.dot(a, b, preferred_element_type=F32)


def _dot_nt(a, b):
    return lax.dot_general(a, b, (((1,), (1,)), ((), ())), preferred_element_type=F32)


def _dot_tn(a, b):
    return lax.dot_general(a, b, (((0,), (0,)), ((), ())), preferred_element_type=F32)


def _post_norm(h, sub, g, b):
    y = DEEPNORM_ALPHA * h + sub
    mu = jnp.mean(y, axis=-1, keepdims=True)
    yc = y - mu
    var = jnp.mean(yc * yc, axis=-1, keepdims=True)
    return yc * lax.rsqrt(var + LN_EPS) * g + b


def _sigmoid(x):
    return 1.0 / (1.0 + jnp.exp(-x))


def _silu(x):
    return x * _sigmoid(x)


FFN_CHUNK = 256


def _ffn_sub(x, wg_ref, wu_ref, wd_ref):
    xb = x.astype(BF16)
    acc = jnp.zeros(x.shape, F32)
    for lo in range(0, wg_ref.shape[1], FFN_CHUNK):
        hi = lo + FFN_CHUNK
        gate = _dot(xb, wg_ref[:, lo:hi])
        up = _dot(xb, wu_ref[:, lo:hi])
        act = (_silu(gate) * up).astype(BF16)
        acc = acc + _dot(act, wd_ref[lo:hi, :])
    return acc


def _mm_body(x_ref, w_ref, o_ref):
    o_ref[...] = _dot(x_ref[...].astype(BF16), w_ref[...]).astype(o_ref.dtype)


def _mem_kv(mem2, wkv_all):
    r, d = mem2.shape
    n = wkv_all.shape[1]
    nb = 2 * d
    tm = min(TM, r)
    return pl.pallas_call(
        _mm_body,
        grid=(n // nb, r // tm),
        in_specs=[pl.BlockSpec((tm, d), lambda j, i: (i, 0)),
                  pl.BlockSpec((d, nb), lambda j, i: (0, j))],
        out_specs=pl.BlockSpec((tm, nb), lambda j, i: (i, j)),
        out_shape=jax.ShapeDtypeStruct((r, n), BF16),
        compiler_params=_cparams("parallel", "parallel"),
        name="memory_kv_proj",
    )(mem2, wkv_all)


def _xattn_sub(x, kv_ref, wq_ref, wo_ref):
    d = x.shape[-1]
    q = _dot(x.astype(BF16), wq_ref[...]) * (X_HEAD_DIM ** -0.5)
    qb = q.astype(BF16)
    outs = []
    for hh in range(X_HEADS):
        lo, hi = hh * X_HEAD_DIM, (hh + 1) * X_HEAD_DIM
        k = kv_ref[0, :, lo:hi]
        v = kv_ref[0, :, d + lo:d + hi]
        s = _dot_nt(qb[:, lo:hi], k)
        m = jnp.max(s, axis=-1, keepdims=True)
        p = jnp.exp(s - m)
        l = jnp.sum(p, axis=-1, keepdims=True)
        outs.append(_dot(p.astype(BF16), v) * (1.0 / l))
    o = jnp.concatenate(outs, axis=-1).astype(BF16)
    return _dot(o, wo_ref[...])


def _gelu_tanh(y):
    return y * (0.5 * (1.0 + jnp.tanh(math.sqrt(2.0 / math.pi) * (y + 0.044715 * (y * y * y)))))


def _tail_body(*refs, even):
    if even:
        (h_ref, a_ref, b_ref, wa_ref, wb_ref, kv_ref, wq_ref, wo_ref, wg_ref, wu_ref, wd_ref, ln_ref, o_ref) = refs
        x = h_ref[0]
        sub = _dot(a_ref[0], wa_ref[...]) + _dot(b_ref[0], wb_ref[...])
    else:
        (h_ref, y_ref, dsk_ref, gwo_ref, gwg_ref, kv_ref, wq_ref, wo_ref, wg_ref, wu_ref, wd_ref, ln_ref,
         o_ref) = refs
        x = h_ref[0]
        hid = _gelu_tanh(y_ref[0].astype(F32) + dsk_ref[...] * x).astype(BF16)
        sub = _dot(hid, gwo_ref[...]) * _sigmoid(_dot(hid, gwg_ref[...]))
    x = _post_norm(x, sub, ln_ref[0:1, :], ln_ref[1:2, :])
    x = _post_norm(x, _xattn_sub(x, kv_ref, wq_ref, wo_ref), ln_ref[2:3, :], ln_ref[3:4, :])
    o_ref[0] = _post_norm(x, _ffn_sub(x, wg_ref, wu_ref, wd_ref), ln_ref[4:5, :], ln_ref[5:6, :])


def _layer_tail(h, mix_inputs, mix_weights, kv_all, layer, wq, wo, wg, wu, wd, ln, *, even):
    bsz, s, d = h.shape
    m = kv_all.shape[1]
    f = wg.shape[1]
    tm = min(TM, s)
    tok = lambda w: pl.BlockSpec((1, tm, w), lambda bi, i: (bi, i, 0))
    in_specs = ([tok(d)] + [tok(t.shape[-1]) for t in mix_inputs] + [_const_spec(w.shape) for w in mix_weights]
                + [pl.BlockSpec((1, m, 2 * d), lambda bi, i: (bi, 0, layer)),
                   _const_spec((d, d)), _const_spec((d, d)),
                   _const_spec((d, f)), _const_spec((d, f)), _const_spec((f, d)), _const_spec(ln.shape)])
    return pl.pallas_call(
        functools.partial(_tail_body, even=even),
        grid=(bsz, s // tm),
        in_specs=in_specs,
        out_specs=tok(d),
        out_shape=jax.ShapeDtypeStruct((bsz, s, d), F32),
        compiler_params=_cparams("parallel", "parallel"),
        name="mixer_out_xattn_swiglu",
    )(h, *mix_inputs, *mix_weights, kv_all, wq, wo, wg, wu, wd, ln)


def _rope_table_body(pos_ref, invf_ref, sgn_ref, cos_ref, sin_ref):
    ang = pos_ref[0] * invf_ref[...]
    cos_ref[0] = jnp.cos(ang)
    sin_ref[0] = jnp.sin(ang) * sgn_ref[...]


def _rope_tables(positions):
    bsz, s = positions.shape
    half = SW_HEAD_DIM // 2
    inv_freq = ROPE_THETA ** (-jnp.arange(0, SW_HEAD_DIM, 2, dtype=F32) / SW_HEAD_DIM)
    invf = jnp.tile(inv_freq, LANES // half)[None, :]
    sgn = jnp.tile(jnp.concatenate([-jnp.ones((half,), F32), jnp.ones((half,), F32)]),
                   LANES // SW_HEAD_DIM)[None, :]
    posf = positions.astype(F32)[..., None]
    tm = min(TM, s)
    return pl.pallas_call(
        _rope_table_body,
        grid=(bsz, s // tm),
        in_specs=[pl.BlockSpec((1, tm, 1), lambda bi, i: (bi, i, 0)),
                  _const_spec((1, LANES)), _const_spec((1, LANES))],
        out_specs=[pl.BlockSpec((1, tm, LANES), lambda bi, i: (bi, i, 0))] * 2,
        out_shape=[jax.ShapeDtypeStruct((bsz, s, LANES), F32)] * 2,
        compiler_params=_cparams("parallel", "parallel"),
        name="rope_tables",
    )(posf, invf, sgn)


_DNZ = DN_QKV_DIM + DN_KEY_DIM
_SW_OFF = _DNZ + GATE_PAD
_HYB_PACKED = _SW_OFF + 3 * SW_DIM


def _hyb_in_body(h_ref, w_ref, cos_ref, sin_ref, dn_ref, gate_ref, q_ref, k_ref, v_ref):
    x = h_ref[0].astype(BF16)
    dn_ref[0] = _dot(x, w_ref[:, :_DNZ]).astype(dn_ref.dtype)
    gate_ref[0] = _dot(x, w_ref[:, _DNZ:_SW_OFF])
    cos = cos_ref[0]
    sin = sin_ref[0]
    lane = lax.broadcasted_iota(jnp.int32, cos.shape, 1)
    first_half = (lane % SW_HEAD_DIM) < (SW_HEAD_DIM // 2)
    half = SW_HEAD_DIM // 2
    for idx, out_ref in enumerate((q_ref, k_ref, v_ref)):
        lo = _SW_OFF + idx * SW_DIM
        y = _dot(x, w_ref[:, lo:lo + SW_DIM])
        for hp in range(SW_DIM // LANES):
            yy = y[:, hp * LANES:(hp + 1) * LANES]
            if idx < 2:
                swapped = jnp.where(first_half, pltpu.roll(yy, LANES - half, 1), pltpu.roll(yy, half, 1))
                yy = yy * cos + swapped * sin
            out_ref[0, hp] = yy


def _hyb_in(h, w_packed, cos, sin):
    bsz, s, d = h.shape
    tm = min(TM, s)
    nhp = SW_DIM // LANES
    qkv_shape = jax.ShapeDtypeStruct((bsz, nhp, s, LANES), F32)
    qkv_spec = pl.BlockSpec((1, nhp, tm, LANES), lambda bi, i: (bi, 0, i, 0))
    tok = lambda w: pl.BlockSpec((1, tm, w), lambda bi, i: (bi, i, 0))
    return pl.pallas_call(
        _hyb_in_body,
        grid=(bsz, s // tm),
        in_specs=[tok(d), _const_spec((d, _HYB_PACKED)), tok(LANES), tok(LANES)],
        out_specs=[tok(_DNZ), tok(GATE_PAD), qkv_spec, qkv_spec, qkv_spec],
        out_shape=[jax.ShapeDtypeStruct((bsz, s, _DNZ), BF16),
                   jax.ShapeDtypeStruct((bsz, s, GATE_PAD), F32),
                   qkv_shape, qkv_shape, qkv_shape],
        compiler_params=_cparams("parallel", "parallel"),
        name="hybrid_in_proj_rope",
    )(h, w_packed, cos, sin)


def _l2n(x):
    return x * lax.rsqrt(jnp.sum(x * x, axis=-1, keepdims=True) + RMS_EPS)


def _deltanet_body(dn_ref, gate_ref, convw_ref, alog_ref, dtb_ref, ng_ref, o_ref, halo, state, *, ts):
    i = pl.program_id(1)
    c = DN_CHUNK
    nchunk = ts // c
    hd = DN_HEAD_DIM

    @pl.when(i == 0)
    def _():
        halo[1] = jnp.zeros(halo.shape[1:], halo.dtype)
        state[...] = jnp.zeros(state.shape, F32)

    x = dn_ref[0]
    xq = x[:, :DN_QKV_DIM]
    row = lax.broadcasted_iota(jnp.int32, (ts, ts), 0)
    col = lax.broadcasted_iota(jnp.int32, (ts, ts), 1)
    nh = halo.shape[1]
    xext = jnp.concatenate([halo[(i + 1) % 2], xq], axis=0)
    erow = lax.broadcasted_iota(jnp.int32, (ts, nh + ts), 0)
    ecol = lax.broadcasted_iota(jnp.int32, (ts, nh + ts), 1)
    y = convw_ref[DN_CONV - 1:DN_CONV, :] * xq.astype(F32)
    for j in range(DN_CONV - 1):
        back = DN_CONV - 1 - j
        shifted = _dot(jnp.where(ecol == erow + (nh - back), 1.0, 0.0).astype(BF16), xext)
        y = y + convw_ref[j:j + 1, :] * shifted
    halo[i % 2] = xq[ts - nh:, :]
    y = _silu(y)

    gt = gate_ref[0]
    beta_all = _sigmoid(gt)
    sp = gt + dtb_ref[...]
    softplus = jnp.maximum(sp, 0.0) + jnp.log(1.0 + jnp.exp(-jnp.abs(sp)))
    g_all = -jnp.exp(alog_ref[...]) * softplus

    same_chunk = (row // c) == (col // c)
    gcum = jnp.dot((same_chunk & (col <= row)).astype(F32), g_all, preferred_element_type=F32,
                   precision=lax.Precision.HIGHEST)
    gtot = jnp.dot(same_chunk.astype(F32), g_all, preferred_element_type=F32, precision=lax.Precision.HIGHEST)
    gcum_t = lax.dot_general(g_all, (same_chunk & (row <= col)).astype(F32), (((0,), (0,)), ((), ())),
                             preferred_element_type=F32, precision=lax.Precision.HIGHEST)
    crow = lax.broadcasted_iota(jnp.int32, (c, c), 0)
    ccol = lax.broadcasted_iota(jnp.int32, (c, c), 1)
    causal = ccol <= crow
    strict = ccol < crow
    eye = (crow == ccol).astype(F32)

    heads = range(DN_HEADS)
    pre = {}
    for h in heads:
        q = _l2n(y[:, h * hd:(h + 1) * hd]) * (hd ** -0.5)
        k = _l2n(y[:, DN_KEY_DIM + h * hd:DN_KEY_DIM + (h + 1) * hd])
        v = y[:, 2 * DN_KEY_DIM + h * hd:2 * DN_KEY_DIM + (h + 1) * hd]
        beta = beta_all[:, h:h + 1]
        gcol = gcum[:, DN_HEADS + h:DN_HEADS + h + 1]
        gl = gtot[:, DN_HEADS + h:DN_HEADS + h + 1]
        kb = k * beta
        egc = jnp.exp(gcol)
        rhs = jnp.concatenate([v * beta, kb * egc], axis=-1).astype(BF16)
        q_dec = q * egc
        k_dec = (k * jnp.exp(gl - gcol)).astype(BF16)
        q16, k16, kb16 = q.astype(BF16), k.astype(BF16), kb.astype(BF16)
        for n in range(nchunk):
            r0, r1 = n * c, (n + 1) * c
            grow = gcum_t[DN_HEADS + h:DN_HEADS + h + 1, r0:r1]
            decay = jnp.exp(jnp.where(causal, gcol[r0:r1] - grow, NEG_BIG))
            low = jnp.where(strict, _dot_nt(kb16[r0:r1], k16[r0:r1]), 0.0) * decay
            a16 = (_dot_nt(q16[r0:r1], k16[r0:r1]) * decay).astype(BF16)
            pre[h, n] = (low, rhs[r0:r1], q_dec[r0:r1], k_dec[r0:r1], a16, jnp.exp(gl[r0:r0 + 1, :]))

    keys = list(pre)
    invs = {kk: eye - pre[kk][0] for kk in keys}
    pws = {kk: _dot(pre[kk][0].astype(BF16), pre[kk][0].astype(BF16)) for kk in keys}
    nfac = (c - 1).bit_length() - 1
    for it in range(nfac):
        pw16 = {kk: pws[kk].astype(BF16) for kk in keys}
        invs = {kk: invs[kk] + _dot(invs[kk].astype(BF16), pw16[kk]) for kk in keys}
        if it < nfac - 1:
            pws = {kk: _dot(pw16[kk], pw16[kk]) for kk in keys}
    sols = {kk: _dot(invs[kk].astype(BF16), pre[kk][1]).astype(BF16) for kk in keys}
    ksol = {kk: _dot_tn(pre[kk][3], sols[kk]) for kk in keys}
    asol = {kk: _dot(pre[kk][4], sols[kk]) for kk in keys}
    qeff = {kk: (pre[kk][2] - asol[kk][:, hd:]).astype(BF16) for kk in keys}

    sts = [state[h] for h in heads]
    outs = [[] for _ in heads]
    for n in range(nchunk):
        for h in heads:
            st16 = sts[h].astype(BF16)
            outs[h].append(_dot(qeff[h, n], st16) + asol[h, n][:, :hd])
            sts[h] = (sts[h] * pre[h, n][5] + ksol[h, n][:, :hd]) - _dot(ksol[h, n][:, hd:].astype(BF16), st16)
    for h in heads:
        state[h] = sts[h]
        o = jnp.concatenate(outs[h], axis=0)
        o = o * lax.rsqrt(jnp.mean(o * o, axis=-1, keepdims=True) + RMS_EPS) * ng_ref[...]
        z = x[:, DN_QKV_DIM + h * hd:DN_QKV_DIM + (h + 1) * hd].astype(F32)
        o_ref[0, :, h * hd:(h + 1) * hd] = (o * _silu(z)).astype(o_ref.dtype)


def _deltanet(dn, gates, conv_w, alog_row, dtb_row, norm_g):
    bsz, s, _ = dn.shape
    ts = min(DN_TILE, s)
    tok = lambda w: pl.BlockSpec((1, ts, w), lambda bi, i: (bi, i, 0))
    return pl.pallas_call(
        functools.partial(_deltanet_body, ts=ts),
        grid=(bsz, s // ts),
        in_specs=[tok(_DNZ), tok(GATE_PAD), _const_spec((DN_CONV, DN_QKV_DIM)),
                  _const_spec((1, GATE_PAD)), _const_spec((1, GATE_PAD)), _const_spec((1, DN_HEAD_DIM))],
        out_specs=tok(DN_KEY_DIM),
        out_shape=jax.ShapeDtypeStruct((bsz, s, DN_KEY_DIM), BF16),
        scratch_shapes=[pltpu.VMEM((2, LANES, DN_QKV_DIM), BF16),
                        pltpu.VMEM((DN_HEADS, DN_HEAD_DIM, DN_HEAD_DIM), F32)],
        compiler_params=_cparams("parallel", "arbitrary"),
        name="gated_deltanet",
    )(dn, gates, conv_w, alog_row, dtb_row, norm_g)


def _dilated_body(q_ref, k_ref, v_ref, o_ref, *bufs, s):
    blk = SW_BLOCK
    obufs, lbufs = bufs[:3], bufs[3:]
    lane = lax.broadcasted_iota(jnp.int32, (blk, LANES), 1)
    head0 = lane < SW_HEAD_DIM
    qi = lax.broadcasted_iota(jnp.int32, (2 * blk, 2 * blk), 0) % blk
    kj = lax.broadcasted_iota(jnp.int32, (2 * blk, 2 * blk), 1)
    dist = blk + qi - kj
    scale = SW_HEAD_DIM ** -0.5

    for br, (window, r) in enumerate(SW_BRANCHES):
        steps = window // r
        band = (dist >= 0) & (dist <= steps)
        nblk = s // (r * blk)
        n_count = min(DIL_UNROLL, nblk)
        rho_count = DIL_UNROLL // n_count
        nrg = r // rho_count

        def rows(start, r=r):
            if r == 1:
                return pl.ds(pl.multiple_of(start, blk), blk)
            return pl.ds(start, blk, stride=r)

        def body(it, carry, r=r, band=band, br=br, rows=rows, n_count=n_count, rho_count=rho_count, nrg=nrg):
            rho_base = (it & (nrg - 1)) * rho_count
            n0 = (it >> (nrg.bit_length() - 1)) * n_count
            starts, kcats, vcats, firsts = [], [], [], []
            for dr in range(rho_count):
                base = rho_base + dr + (r * blk) * n0
                pstart = jnp.maximum(base - r * blk, 0)
                kprev = k_ref[0, 0, rows(pstart), :].astype(BF16)
                vprev = v_ref[0, 0, rows(pstart), :].astype(BF16)
                for dn in range(n_count):
                    start = base + (r * blk) * dn
                    kcur = k_ref[0, 0, rows(start), :].astype(BF16)
                    vcur = v_ref[0, 0, rows(start), :].astype(BF16)
                    starts.append(start)
                    kcats.append(jnp.concatenate([kprev, kcur], axis=0))
                    vcats.append(jnp.concatenate([vprev, vcur], axis=0))
                    firsts.append(dn == 0)
                    kprev, vprev = kcur, vcur
            scs = []
            for start, kcat in zip(starts, kcats):
                qb = q_ref[0, 0, rows(start), :] * scale
                q2 = jnp.concatenate([jnp.where(head0, qb, 0.0), jnp.where(head0, 0.0, qb)], axis=0)
                scs.append(_dot_nt(q2.astype(BF16), kcat))
            first_valid = band & ((n0 > 0) | (kj >= blk))
            scs = [jnp.where(first_valid if first else band, sc, NEG_BIG) for sc, first in zip(scs, firsts)]
            ms = [jnp.max(sc, axis=-1, keepdims=True) for sc in scs]
            ps = [jnp.exp(sc - m) for sc, m in zip(scs, ms)]
            ls = [jnp.sum(p, axis=-1, keepdims=True) for p in ps]
            o2s = [_dot(p.astype(BF16), vcat) * (1.0 / l) for p, vcat, l in zip(ps, vcats, ls)]
            for start, o2, m, l in zip(starts, o2s, ms, ls):
                lse2 = m + jnp.log(l)
                obufs[br][rows(start), :] = jnp.where(head0, o2[:blk], o2[blk:])
                lbufs[br][rows(start), :] = jnp.where(head0, lse2[:blk], lse2[blk:])
            return carry

        lax.fori_loop(0, nrg * (nblk // n_count), body, 0)

    rows_per = 4 * blk

    def combine(ci, carry):
        sl = pl.ds(pl.multiple_of(ci * rows_per, rows_per), rows_per)
        l0, l1, l2 = lbufs[0][sl, :], lbufs[1][sl, :], lbufs[2][sl, :]
        mx = jnp.maximum(jnp.maximum(l0, l1), l2)
        w0, w1, w2 = jnp.exp(l0 - mx), jnp.exp(l1 - mx), jnp.exp(l2 - mx)
        out = (obufs[0][sl, :] * w0 + obufs[1][sl, :] * w1 + obufs[2][sl, :] * w2) / (w0 + w1 + w2)
        o_ref[0, sl, :] = out.astype(o_ref.dtype)
        return carry

    lax.fori_loop(0, s // rows_per, combine, 0)


def _dilated_attention(q, k, v):
    bsz, nhp, s, _ = q.shape
    spec = pl.BlockSpec((1, 1, s, LANES), lambda bi, hp: (bi, hp, 0, 0))
    return pl.pallas_call(
        functools.partial(_dilated_body, s=s),
        grid=(bsz, nhp),
        in_specs=[spec, spec, spec],
        out_specs=pl.BlockSpec((1, s, LANES), lambda bi, hp: (bi, 0, hp)),
        out_shape=jax.ShapeDtypeStruct((bsz, s, SW_DIM), BF16),
        scratch_shapes=[pltpu.VMEM((s, LANES), F32)] * 6,
        compiler_params=_cparams("parallel", "parallel"),
        name="dilated_window_attention",
    )(q, k, v)


S5_PITCH = 40


def _s5_body(h_ref, toe_ref, wis_ref, wso_ref, apr_ref, api_ref, y_ref, xpad, ypad, ubuf, ybuf, *, ncb):
    l = S5_CHUNK
    gsz = S5_GROUP
    p = S5_STATE
    ngroups = LANES // gsz

    def copy_in(c, carry):
        xpad[pl.ds(pl.multiple_of(c * S5_PITCH, 8), l), :] = h_ref[0, pl.ds(pl.multiple_of(c * l, l), l), :]
        return carry

    lax.fori_loop(0, ncb, copy_in, 0, unroll=8)

    def gather(si, carry):
        ubuf[si] = xpad[pl.ds(si, ncb, stride=S5_PITCH), :].T.astype(BF16)
        return carry

    lax.fori_loop(0, l, gather, 0, unroll=4)

    lane = lax.broadcasted_iota(jnp.int32, (p, ncb), 1)
    nsteps = (ncb - 1).bit_length()
    for gi in range(ngroups):
        u = ubuf[:, gi * gsz:(gi + 1) * gsz, :].reshape(l * gsz, ncb)
        y_intra = _dot(toe_ref[gi], u)
        s_loc = _dot(wis_ref[gi], u)
        apr = apr_ref[gi]
        api = api_ref[gi]
        zr = s_loc[:p]
        zi = s_loc[p:]
        for kk in range(nsteps):
            sh = 1 << kk
            zr_s = pltpu.roll(zr, sh, 1)
            zi_s = pltpu.roll(zi, sh, 1)
            ar = apr[:, kk:kk + 1]
            ai = api[:, kk:kk + 1]
            ok = lane >= sh
            zr, zi = (zr + jnp.where(ok, ar * zr_s - ai * zi_s, 0.0),
                      zi + jnp.where(ok, ar * zi_s + ai * zr_s, 0.0))
        x_in = jnp.concatenate([jnp.where(lane >= 1, pltpu.roll(zr, 1, 1), 0.0),
                                jnp.where(lane >= 1, pltpu.roll(zi, 1, 1), 0.0)], axis=0)
        y = y_intra + _dot(wso_ref[gi], x_in.astype(BF16))
        ybuf[:, gi * gsz:(gi + 1) * gsz, :] = y.reshape(l, gsz, ncb)

    def scatter(si, carry):
        ypad[pl.ds(si, ncb, stride=S5_PITCH), :] = ybuf[si].T
        return carry

    lax.fori_loop(0, l, scatter, 0, unroll=4)

    def copy_out(c, carry):
        y_ref[0, pl.ds(pl.multiple_of(c * l, l), l), :] = (
            ypad[pl.ds(pl.multiple_of(c * S5_PITCH, 8), l), :].astype(y_ref.dtype))
        return carry

    lax.fori_loop(0, ncb, copy_out, 0, unroll=8)


def _s5_conv(h, toe, wis, wso, apr, api):
    bsz, s, d = h.shape
    l = S5_CHUNK
    ncb = s // l
    ngroups = LANES // S5_GROUP
    lg = l * S5_GROUP
    p2 = 2 * S5_STATE
    nk = apr.shape[-1]
    per_tile = lambda a, b: pl.BlockSpec((ngroups, a, b), lambda j, bi: (j, 0, 0))
    tok = pl.BlockSpec((1, s, LANES), lambda j, bi: (bi, 0, j))
    return pl.pallas_call(
        functools.partial(_s5_body, ncb=ncb),
        grid=(d // LANES, bsz),
        in_specs=[tok, per_tile(lg, lg), per_tile(p2, lg), per_tile(lg, p2),
                  per_tile(S5_STATE, nk), per_tile(S5_STATE, nk)],
        out_specs=tok,
        out_shape=jax.ShapeDtypeStruct((bsz, s, d), BF16),
        scratch_shapes=[pltpu.VMEM((ncb * S5_PITCH, LANES), F32), pltpu.VMEM((ncb * S5_PITCH, LANES), F32),
                        pltpu.VMEM((l, LANES, ncb), BF16), pltpu.VMEM((l, LANES, ncb), F32)],
        compiler_params=_cparams("parallel", "parallel"),
        name="s5_chunked_conv",
    )(h, toe, wis, wso, apr, api)


def _s5_tables(a_re, a_im, log_dt, b_re, b_im, c_re, c_im, ncb):
    l = S5_CHUNK
    a = lax.complex(a_re.astype(F32), a_im.astype(F32))
    dt = jnp.exp(log_dt.astype(F32))[:, None]
    adt = a * dt
    a_bar = jnp.exp(adt)
    b_bar = ((a_bar - 1.0) / a)[..., None] * lax.complex(b_re.astype(F32), b_im.astype(F32))
    cc = lax.complex(c_re.astype(F32), c_im.astype(F32))
    j = jnp.arange(l + 1, dtype=F32)
    apow = jnp.exp(adt[None] * j[:, None, None])
    kern = jnp.einsum('ghp,jgp,gpi->gjhi', cc, apow[:l], b_bar).real
    g = a.shape[0]
    kext = jnp.concatenate([kern[:, ::-1], jnp.zeros((g, l - 1, S5_GROUP, S5_GROUP), F32)], axis=1)
    kext = kext.transpose(0, 2, 1, 3).reshape(g, S5_GROUP, (2 * l - 1) * S5_GROUP).astype(BF16)
    toe = jnp.stack([kext[:, :, (l - 1 - t) * S5_GROUP:(2 * l - 1 - t) * S5_GROUP] for t in range(l)], axis=1)
    toe = toe.reshape(g, l * S5_GROUP, l * S5_GROUP)
    w_in = apow[:l][::-1].transpose(1, 2, 0)[..., None] * b_bar[:, :, None, :]
    w_in = w_in.reshape(g, S5_STATE, l * S5_GROUP)
    wis = jnp.concatenate([w_in.real, w_in.imag], axis=1)
    w_out = cc[:, None, :, :] * apow[1:l + 1].transpose(1, 0, 2)[:, :, None, :]
    w_out = w_out.reshape(g, l * S5_GROUP, S5_STATE)
    wso = jnp.concatenate([w_out.real, -w_out.imag], axis=2)
    nsteps = (ncb - 1).bit_length()
    kexp = (l * (2.0 ** jnp.arange(nsteps, dtype=F32)))
    ak = jnp.exp(adt[..., None] * kexp)
    return toe.astype(BF16), wis.astype(BF16), wso.astype(BF16), ak.real, ak.imag


def _even_mixer(h, cos, sin, w_in, conv_w, a_log, dt_bias, norm_g, w_out):
    w_gate = jnp.pad(w_in[:, _DNZ:_DNZ + 2 * DN_HEADS], ((0, 0), (0, GATE_PAD - 2 * DN_HEADS)))
    w_packed = jnp.concatenate([w_in[:, :_DNZ], w_gate, w_in[:, _DNZ + 2 * DN_HEADS:]], axis=1).astype(BF16)
    dn, gates, q, k, v = _hyb_in(h, w_packed, cos, sin)
    lane_pad = lambda p: jnp.pad(p.astype(F32), (DN_HEADS, GATE_PAD - 2 * DN_HEADS))[None, :]
    a_out = _deltanet(dn, gates, conv_w.astype(F32), lane_pad(a_log), lane_pad(dt_bias),
                      norm_g.astype(F32)[None, :])
    b_out = _dilated_attention(q, k, v)
    w16 = w_out.astype(BF16)
    return (a_out, b_out), (w16[:DN_KEY_DIM], w16[DN_KEY_DIM:])


def _odd_mixer(h, a_re, a_im, log_dt, b_re, b_im, c_re, c_im, d_skip, w_o, w_g):
    s = h.shape[1]
    toe, wis, wso, apr, api = _s5_tables(a_re, a_im, log_dt, b_re, b_im, c_re, c_im, s // S5_CHUNK)
    y = _s5_conv(h, toe, wis, wso, apr, api)
    return (y,), (d_skip.astype(F32)[None, :], w_o.astype(BF16), w_g.astype(BF16))


def kernel(x, mem, positions, hyb_w_in, dn_conv_w, dn_a_log, dn_dt_bias, dn_norm_g, hyb_w_out, s5_a_re, s5_a_im, s5_log_dt, s5_b_re, s5_b_im, s5_c_re, s5_c_im, s5_d, s5_glu_wo, s5_glu_wg, ln_mix_g, ln_mix_b, xq_w, xk_w, xv_w, xo_w, ln_x_g, ln_x_b, ffn_wg, ffn_wu, ffn_wd, ln_ffn_g, ln_ffn_b):
    bsz, s, d = x.shape
    m = mem.shape[1]
    depth = xq_w.shape[0]
    cos, sin = _rope_tables(positions)
    wkv_all = jnp.concatenate([jnp.concatenate([xk_w[l], xv_w[l]], axis=1) for l in range(depth)],
                              axis=1).astype(BF16)
    kv_all = _mem_kv(mem.reshape(bsz * m, d), wkv_all).reshape(bsz, m, depth * 2 * d)
    h = x
    for layer in range(depth):
        i = layer // 2
        even = layer % 2 == 0
        if even:
            mix_inputs, mix_weights = _even_mixer(h, cos, sin, hyb_w_in[i], dn_conv_w[i], dn_a_log[i],
                                                  dn_dt_bias[i], dn_norm_g[i], hyb_w_out[i])
        else:
            mix_inputs, mix_weights = _odd_mixer(h, s5_a_re[i], s5_a_im[i], s5_log_dt[i], s5_b_re[i], s5_b_im[i],
                                                 s5_c_re[i], s5_c_im[i], s5_d[i], s5_glu_wo[i], s5_glu_wg[i])
        ln = jnp.stack([ln_mix_g[layer], ln_mix_b[layer], ln_x_g[layer], ln_x_b[layer],
                        ln_ffn_g[layer], ln_ffn_b[layer]]).astype(F32)
        h = _layer_tail(h, mix_inputs, mix_weights, kv_all, layer, xq_w[layer].astype(BF16),
                        xo_w[layer].astype(BF16), ffn_wg[layer].astype(BF16), ffn_wu[layer].astype(BF16),
                        ffn_wd[layer].astype(BF16), ln, even=even)
    return h
```

```python
import functools
import math

import jax
import jax.numpy as jnp
from jax import lax
from jax.experimental import pallas as pl
from jax.experimental.pallas import tpu as pltpu

F32 = jnp.float32
BF16 = jnp.bfloat16

D_MODEL = 1024
DEPTH = 4
DN_HEADS = 4
DN_HEAD_DIM = 128
DN_KEY_DIM = DN_HEADS * DN_HEAD_DIM
DN_QKV_DIM = 3 * DN_KEY_DIM
DN_CONV = 4
DN_CHUNK = 128
SW_HEADS = 8
SW_HEAD_DIM = 64
SW_DIM = SW_HEADS * SW_HEAD_DIM
SW_BRANCHES = ((128, 1), (512, 4), (2048, 16))
SW_BLOCK = 128
ROPE_THETA = 10000.0
S5_GROUP = 16
S5_GROUPS = D_MODEL // S5_GROUP
S5_STATE = 64
S5_CHUNK = 32
X_HEADS = 4
X_HEAD_DIM = D_MODEL // X_HEADS
DEEPNORM_ALPHA = (2 * DEPTH) ** 0.25
LN_EPS = 1e-5
RMS_EPS = 1e-6

LANES = 128
GATE_PAD = LANES
VMEM_LIMIT = 56 * 1024 * 1024
NEG_BIG = -1e30

TM = 512
DN_TILE = 256
DIL_UNROLL = 4


def _cparams(*sem):
    return pltpu.CompilerParams(dimension_semantics=sem, vmem_limit_bytes=VMEM_LIMIT)


def _const_spec(shape):
    nd = len(shape)
    return pl.BlockSpec(shape, lambda *_: (0,) * nd, pipeline_mode=pl.Buffered(1))


def _dot(a, b):
    return jnp.dot(a, b, preferred_element_type=F32)


def _dot_nt(a, b):
    return lax.dot_general(a, b, (((1,), (1,)), ((), ())), preferred_element_type=F32)


def _dot_tn(a, b):
    return lax.dot_general(a, b, (((0,), (0,)), ((), ())), preferred_element_type=F32)


def _post_norm(h, sub, g, b):
    y = DEEPNORM_ALPHA * h + sub
    mu = jnp.mean(y, axis=-1, keepdims=True)
    yc = y - mu
    var = jnp.mean(yc * yc, axis=-1, keepdims=True)
    return yc * lax.rsqrt(var + LN_EPS) * g + b


def _sigmoid(x):
    return 1.0 / (1.0 + jnp.exp(-x))


def _silu(x):
    return x * _sigmoid(x)


FFN_CHUNK = 256


def _ffn_sub(x, wg_ref, wu_ref, wd_ref):
    xb = x.astype(BF16)
    acc = jnp.zeros(x.shape, F32)
    for lo in range(0, wg_ref.shape[1], FFN_CHUNK):
        hi = lo + FFN_CHUNK
        gate = _dot(xb, wg_ref[:, lo:hi])
        up = _dot(xb, wu_ref[:, lo:hi])
        act = (_silu(gate) * up).astype(BF16)
        acc = acc + _dot(act, wd_ref[lo:hi, :])
    return acc


def _mm_body(x_ref, w_ref, o_ref):
    o_ref[...] = _dot(x_ref[...].astype(BF16), w_ref[...]).astype(o_ref.dtype)


def _mem_kv(mem2, wkv_all):
    r, d = mem2.shape
    n = wkv_all.shape[1]
    nb = 2 * d
    tm = min(TM, r)
    return pl.pallas_call(
        _mm_body,
        grid=(n // nb, r // tm),
        in_specs=[pl.BlockSpec((tm, d), lambda j, i: (i, 0)),
                  pl.BlockSpec((d, nb), lambda j, i: (0, j))],
        out_specs=pl.BlockSpec((tm, nb), lambda j, i: (i, j)),
        out_shape=jax.ShapeDtypeStruct((r, n), BF16),
        compiler_params=_cparams("parallel", "parallel"),
        name="memory_kv_proj",
    )(mem2, wkv_all)


def _xattn_sub(x, kv_ref, wq_ref, wo_ref):
    d = x.shape[-1]
    q = _dot(x.astype(BF16), wq_ref[...]) * (X_HEAD_DIM ** -0.5)
    qb = q.astype(BF16)
    outs = []
    for hh in range(X_HEADS):
        lo, hi = hh * X_HEAD_DIM, (hh + 1) * X_HEAD_DIM
        k = kv_ref[0, :, lo:hi]
        v = kv_ref[0, :, d + lo:d + hi]
        s = _dot_nt(qb[:, lo:hi], k)
        m = jnp.max(s, axis=-1, keepdims=True)
        p = jnp.exp(s - m)
        l = jnp.sum(p, axis=-1, keepdims=True)
        outs.append(_dot(p.astype(BF16), v) * (1.0 / l))
    o = jnp.concatenate(outs, axis=-1).astype(BF16)
    return _dot(o, wo_ref[...])


def _gelu_tanh(y):
    return y * (0.5 * (1.0 + jnp.tanh(math.sqrt(2.0 / math.pi) * (y + 0.044715 * (y * y * y)))))


def _tail_body(*refs, even):
    if even:
        (h_ref, a_ref, b_ref, wa_ref, wb_ref, kv_ref, wq_ref, wo_ref, wg_ref, wu_ref, wd_ref, ln_ref, o_ref) = refs
        x = h_ref[0]
        sub = _dot(a_ref[0], wa_ref[...]) + _dot(b_ref[0], wb_ref[...])
    else:
        (h_ref, y_ref, dsk_ref, gwo_ref, gwg_ref, kv_ref, wq_ref, wo_ref, wg_ref, wu_ref, wd_ref, ln_ref,
         o_ref) = refs
        x = h_ref[0]
        hid = _gelu_tanh(y_ref[0].astype(F32) + dsk_ref[...] * x).astype(BF16)
        sub = _dot(hid, gwo_ref[...]) * _sigmoid(_dot(hid, gwg_ref[...]))
    x = _post_norm(x, sub, ln_ref[0:1, :], ln_ref[1:2, :])
    x = _post_norm(x, _xattn_sub(x, kv_ref, wq_ref, wo_ref), ln_ref[2:3, :], ln_ref[3:4, :])
    o_ref[0] = _post_norm(x, _ffn_sub(x, wg_ref, wu_ref, wd_ref), ln_ref[4:5, :], ln_ref[5:6, :])


def _layer_tail(h, mix_inputs, mix_weights, kv_all, layer, wq, wo, wg, wu, wd, ln, *, even):
    bsz, s, d = h.shape
    m = kv_all.shape[1]
    f = wg.shape[1]
    tm = min(TM, s)
    tok = lambda w: pl.BlockSpec((1, tm, w), lambda bi, i: (bi, i, 0))
    in_specs = ([tok(d)] + [tok(t.shape[-1]) for t in mix_inputs] + [_const_spec(w.shape) for w in mix_weights]
                + [pl.BlockSpec((1, m, 2 * d), lambda bi, i: (bi, 0, layer)),
                   _const_spec((d, d)), _const_spec((d, d)),
                   _const_spec((d, f)), _const_spec((d, f)), _const_spec((f, d)), _const_spec(ln.shape)])
    return pl.pallas_call(
        functools.partial(_tail_body, even=even),
        grid=(bsz, s // tm),
        in_specs=in_specs,
        out_specs=tok(d),
        out_shape=jax.ShapeDtypeStruct((bsz, s, d), F32),
        compiler_params=_cparams("parallel", "parallel"),
        name="mixer_out_xattn_swiglu",
    )(h, *mix_inputs, *mix_weights, kv_all, wq, wo, wg, wu, wd, ln)


def _rope_table_body(pos_ref, invf_ref, sgn_ref, cos_ref, sin_ref):
    ang = pos_ref[0] * invf_ref[...]
    cos_ref[0] = jnp.cos(ang)
    sin_ref[0] = jnp.sin(ang) * sgn_ref[...]


def _rope_tables(positions):
    bsz, s = positions.shape
    half = SW_HEAD_DIM // 2
    inv_freq = ROPE_THETA ** (-jnp.arange(0, SW_HEAD_DIM, 2, dtype=F32) / SW_HEAD_DIM)
    invf = jnp.tile(inv_freq, LANES // half)[None, :]
    sgn = jnp.tile(jnp.concatenate([-jnp.ones((half,), F32), jnp.ones((half,), F32)]),
                   LANES // SW_HEAD_DIM)[None, :]
    posf = positions.astype(F32)[..., None]
    tm = min(TM, s)
    return pl.pallas_call(
        _rope_table_body,
        grid=(bsz, s // tm),
        in_specs=[pl.BlockSpec((1, tm, 1), lambda bi, i: (bi, i, 0)),
                  _const_spec((1, LANES)), _const_spec((1, LANES))],
        out_specs=[pl.BlockSpec((1, tm, LANES), lambda bi, i: (bi, i, 0))] * 2,
        out_shape=[jax.ShapeDtypeStruct((bsz, s, LANES), F32)] * 2,
        compiler_params=_cparams("parallel", "parallel"),
        name="rope_tables",
    )(posf, invf, sgn)


_DNZ = DN_QKV_DIM + DN_KEY_DIM
_SW_OFF = _DNZ + GATE_PAD
_HYB_PACKED = _SW_OFF + 3 * SW_DIM


def _hyb_in_body(h_ref, w_ref, cos_ref, sin_ref, dn_ref, gate_ref, q_ref, k_ref, v_ref):
    x = h_ref[0].astype(BF16)
    dn_ref[0] = _dot(x, w_ref[:, :_DNZ]).astype(dn_ref.dtype)
    gate_ref[0] = _dot(x, w_ref[:, _DNZ:_SW_OFF])
    cos = cos_ref[0]
    sin = sin_ref[0]
    lane = lax.broadcasted_iota(jnp.int32, cos.shape, 1)
    first_half = (lane % SW_HEAD_DIM) < (SW_HEAD_DIM // 2)
    half = SW_HEAD_DIM // 2
    for idx, out_ref in enumerate((q_ref, k_ref, v_ref)):
        lo = _SW_OFF + idx * SW_DIM
        y = _dot(x, w_ref[:, lo:lo + SW_DIM])
        for hp in range(SW_DIM // LANES):
            yy = y[:, hp * LANES:(hp + 1) * LANES]
            if idx < 2:
                swapped = jnp.where(first_half, pltpu.roll(yy, LANES - half, 1), pltpu.roll(yy, half, 1))
                yy = yy * cos + swapped * sin
            out_ref[0, hp] = yy


def _hyb_in(h, w_packed, cos, sin):
    bsz, s, d = h.shape
    tm = min(TM, s)
    nhp = SW_DIM // LANES
    qkv_shape = jax.ShapeDtypeStruct((bsz, nhp, s, LANES), F32)
    qkv_spec = pl.BlockSpec((1, nhp, tm, LANES), lambda bi, i: (bi, 0, i, 0))
    tok = lambda w: pl.BlockSpec((1, tm, w), lambda bi, i: (bi, i, 0))
    return pl.pallas_call(
        _hyb_in_body,
        grid=(bsz, s // tm),
        in_specs=[tok(d), _const_spec((d, _HYB_PACKED)), tok(LANES), tok(LANES)],
        out_specs=[tok(_DNZ), tok(GATE_PAD), qkv_spec, qkv_spec, qkv_spec],
        out_shape=[jax.ShapeDtypeStruct((bsz, s, _DNZ), BF16),
                   jax.ShapeDtypeStruct((bsz, s, GATE_PAD), F32),
                   qkv_shape, qkv_shape, qkv_shape],
        compiler_params=_cparams("parallel", "parallel"),
        name="hybrid_in_proj_rope",
    )(h, w_packed, cos, sin)


def _l2n(x):
    return x * lax.rsqrt(jnp.sum(x * x, axis=-1, keepdims=True) + RMS_EPS)


def _deltanet_body(dn_ref, gate_ref, convw_ref, alog_ref, dtb_ref, ng_ref, o_ref, halo, state, *, ts):
    i = pl.program_id(1)
    c = DN_CHUNK
    nchunk = ts // c
    hd = DN_HEAD_DIM

    @pl.when(i == 0)
    def _():
        halo[1] = jnp.zeros(halo.shape[1:], halo.dtype)
        state[...] = jnp.zeros(state.shape, F32)

    x = dn_ref[0]
    xq = x[:, :DN_QKV_DIM]
    row = lax.broadcasted_iota(jnp.int32, (ts, ts), 0)
    col = lax.broadcasted_iota(jnp.int32, (ts, ts), 1)
    nh = halo.shape[1]
    xext = jnp.concatenate([halo[(i + 1) % 2], xq], axis=0)
    erow = lax.broadcasted_iota(jnp.int32, (ts, nh + ts), 0)
    ecol = lax.broadcasted_iota(jnp.int32, (ts, nh + ts), 1)
    y = convw_ref[DN_CONV - 1:DN_CONV, :] * xq.astype(F32)
    for j in range(DN_CONV - 1):
        back = DN_CONV - 1 - j
        shifted = _dot(jnp.where(ecol == erow + (nh - back), 1.0, 0.0).astype(BF16), xext)
        y = y + convw_ref[j:j + 1, :] * shifted
    halo[i % 2] = xq[ts - nh:, :]
    y = _silu(y)

    gt = gate_ref[0]
    beta_all = _sigmoid(gt)
    sp = gt + dtb_ref[...]
    softplus = jnp.maximum(sp, 0.0) + jnp.log(1.0 + jnp.exp(-jnp.abs(sp)))
    g_all = -jnp.exp(alog_ref[...]) * softplus

    same_chunk = (row // c) == (col // c)
    gcum = jnp.dot((same_chunk & (col <= row)).astype(F32), g_all, preferred_element_type=F32,
                   precision=lax.Precision.HIGHEST)
    gtot = jnp.dot(same_chunk.astype(F32), g_all, preferred_element_type=F32, precision=lax.Precision.HIGHEST)
    gcum_t = lax.dot_general(g_all, (same_chunk & (row <= col)).astype(F32), (((0,), (0,)), ((), ())),
                             preferred_element_type=F32, precision=lax.Precision.HIGHEST)
    crow = lax.broadcasted_iota(jnp.int32, (c, c), 0)
    ccol = lax.broadcasted_iota(jnp.int32, (c, c), 1)
    causal = ccol <= crow
    strict = ccol < crow
    eye = (crow == ccol).astype(F32)

    heads = range(DN_HEADS)
    pre = {}
    for h in heads:
        q = _l2n(y[:, h * hd:(h + 1) * hd]) * (hd ** -0.5)
        k = _l2n(y[:, DN_KEY_DIM + h * hd:DN_KEY_DIM + (h + 1) * hd])
        v = y[:, 2 * DN_KEY_DIM + h * hd:2 * DN_KEY_DIM + (h + 1) * hd]
        beta = beta_all[:, h:h + 1]
        gcol = gcum[:, DN_HEADS + h:DN_HEADS + h + 1]
        gl = gtot[:, DN_HEADS + h:DN_HEADS + h + 1]
        kb = k * beta
        egc = jnp.exp(gcol)
        rhs = jnp.concatenate([v * beta, kb * egc], axis=-1).astype(BF16)
        q_dec = q * egc
        k_dec = (k * jnp.exp(gl - gcol)).astype(BF16)
        q16, k16, kb16 = q.astype(BF16), k.astype(BF16), kb.astype(BF16)
        for n in range(nchunk):
            r0, r1 = n * c, (n + 1) * c
            grow = gcum_t[DN_HEADS + h:DN_HEADS + h + 1, r0:r1]
            decay = jnp.exp(jnp.where(causal, gcol[r0:r1] - grow, NEG_BIG))
            low = jnp.where(strict, _dot_nt(kb16[r0:r1], k16[r0:r1]), 0.0) * decay
            a16 = (_dot_nt(q16[r0:r1], k16[r0:r1]) * decay).astype(BF16)
            pre[h, n] = (low, rhs[r0:r1], q_dec[r0:r1], k_dec[r0:r1], a16, jnp.exp(gl[r0:r0 + 1, :]))

    keys = list(pre)
    half = c // 2
    diag_half = (crow // half) == (ccol // half)
    lows_d = {kk: jnp.where(diag_half, pre[kk][0], 0.0) for kk in keys}
    lows_o = {kk: jnp.where(diag_half, 0.0, pre[kk][0]).astype(BF16) for kk in keys}
    invs = {kk: eye - lows_d[kk] for kk in keys}
    pws = {kk: _dot(lows_d[kk].astype(BF16), lows_d[kk].astype(BF16)) for kk in keys}
    nfac = (half - 1).bit_length() - 1
    for it in range(nfac):
        pw16 = {kk: pws[kk].astype(BF16) for kk in keys}
        invs = {kk: invs[kk] + _dot(invs[kk].astype(BF16), pw16[kk]) for kk in keys}
        if it < nfac - 1:
            pws = {kk: _dot(pw16[kk], pw16[kk]) for kk in keys}
    inv16 = {kk: invs[kk].astype(BF16) for kk in keys}
    cross = {kk: _dot(inv16[kk], lows_o[kk]).astype(BF16) for kk in keys}
    invs = {kk: invs[kk] - _dot(cross[kk], inv16[kk]) for kk in keys}
    sols = {kk: _dot(invs[kk].astype(BF16), pre[kk][1]).astype(BF16) for kk in keys}
    ksol = {kk: _dot_tn(pre[kk][3], sols[kk]) for kk in keys}
    asol = {kk: _dot(pre[kk][4], sols[kk]) for kk in keys}
    qeff = {kk: (pre[kk][2] - asol[kk][:, hd:]).astype(BF16) for kk in keys}

    sts = [state[h] for h in heads]
    outs = [[] for _ in heads]
    for n in range(nchunk):
        for h in heads:
            st16 = sts[h].astype(BF16)
            outs[h].append(_dot(qeff[h, n], st16) + asol[h, n][:, :hd])
            sts[h] = (sts[h] * pre[h, n][5] + ksol[h, n][:, :hd]) - _dot(ksol[h, n][:, hd:].astype(BF16), st16)
    for h in heads:
        state[h] = sts[h]
        o = jnp.concatenate(outs[h], axis=0)
        o = o * lax.rsqrt(jnp.mean(o * o, axis=-1, keepdims=True) + RMS_EPS) * ng_ref[...]
        z = x[:, DN_QKV_DIM + h * hd:DN_QKV_DIM + (h + 1) * hd].astype(F32)
        o_ref[0, :, h * hd:(h + 1) * hd] = (o * _silu(z)).astype(o_ref.dtype)


def _deltanet(dn, gates, conv_w, alog_row, dtb_row, norm_g):
    bsz, s, _ = dn.shape
    ts = min(DN_TILE, s)
    tok = lambda w: pl.BlockSpec((1, ts, w), lambda bi, i: (bi, i, 0))
    return pl.pallas_call(
        functools.partial(_deltanet_body, ts=ts),
        grid=(bsz, s // ts),
        in_specs=[tok(_DNZ), tok(GATE_PAD), _const_spec((DN_CONV, DN_QKV_DIM)),
                  _const_spec((1, GATE_PAD)), _const_spec((1, GATE_PAD)), _const_spec((1, DN_HEAD_DIM))],
        out_specs=tok(DN_KEY_DIM),
        out_shape=jax.ShapeDtypeStruct((bsz, s, DN_KEY_DIM), BF16),
        scratch_shapes=[pltpu.VMEM((2, LANES, DN_QKV_DIM), BF16),
                        pltpu.VMEM((DN_HEADS, DN_HEAD_DIM, DN_HEAD_DIM), F32)],
        compiler_params=_cparams("parallel", "arbitrary"),
        name="gated_deltanet",
    )(dn, gates, conv_w, alog_row, dtb_row, norm_g)


def _dilated_body(q_ref, k_ref, v_ref, o_ref, *bufs, s):
    blk = SW_BLOCK
    obufs, lbufs = bufs[:3], bufs[3:]
    lane = lax.broadcasted_iota(jnp.int32, (blk, LANES), 1)
    head0 = lane < SW_HEAD_DIM
    qi = lax.broadcasted_iota(jnp.int32, (2 * blk, 2 * blk), 0) % blk
    kj = lax.broadcasted_iota(jnp.int32, (2 * blk, 2 * blk), 1)
    dist = blk + qi - kj
    scale = SW_HEAD_DIM ** -0.5

    for br, (window, r) in enumerate(SW_BRANCHES):
        steps = window // r
        band = (dist >= 0) & (dist <= steps)
        nblk = s // (r * blk)
        n_count = min(DIL_UNROLL, nblk)
        rho_count = DIL_UNROLL // n_count
        nrg = r // rho_count

        def rows(start, r=r):
            if r == 1:
                return pl.ds(pl.multiple_of(start, blk), blk)
            return pl.ds(start, blk, stride=r)

        def body(it, carry, r=r, band=band, br=br, rows=rows, n_count=n_count, rho_count=rho_count, nrg=nrg):
            rho_base = (it & (nrg - 1)) * rho_count
            n0 = (it >> (nrg.bit_length() - 1)) * n_count
            starts, kcats, vcats, firsts = [], [], [], []
            for dr in range(rho_count):
                base = rho_base + dr + (r * blk) * n0
                pstart = jnp.maximum(base - r * blk, 0)
                kprev = k_ref[0, 0, rows(pstart), :].astype(BF16)
                vprev = v_ref[0, 0, rows(pstart), :].astype(BF16)
                for dn in range(n_count):
                    start = base + (r * blk) * dn
                    kcur = k_ref[0, 0, rows(start), :].astype(BF16)
                    vcur = v_ref[0, 0, rows(start), :].astype(BF16)
                    starts.append(start)
                    kcats.append(jnp.concatenate([kprev, kcur], axis=0))
                    vcats.append(jnp.concatenate([vprev, vcur], axis=0))
                    firsts.append(dn == 0)
                    kprev, vprev = kcur, vcur
            scs = []
            for start, kcat in zip(starts, kcats):
                qb = q_ref[0, 0, rows(start), :] * scale
                q2 = jnp.concatenate([jnp.where(head0, qb, 0.0), jnp.where(head0, 0.0, qb)], axis=0)
                scs.append(_dot_nt(q2.astype(BF16), kcat))
            first_valid = band & ((n0 > 0) | (kj >= blk))
            scs = [jnp.where(first_valid if first else band, sc, NEG_BIG) for sc, first in zip(scs, firsts)]
            ms = [jnp.max(sc, axis=-1, keepdims=True) for sc in scs]
            ps = [jnp.exp(sc - m) for sc, m in zip(scs, ms)]
            ls = [jnp.sum(p, axis=-1, keepdims=True) for p in ps]
            o2s = [_dot(p.astype(BF16), vcat) * (1.0 / l) for p, vcat, l in zip(ps, vcats, ls)]
            for start, o2, m, l in zip(starts, o2s, ms, ls):
                lse2 = m + jnp.log(l)
                obufs[br][rows(start), :] = jnp.where(head0, o2[:blk], o2[blk:])
                lbufs[br][rows(start), :] = jnp.where(head0, lse2[:blk], lse2[blk:])
            return carry

        lax.fori_loop(0, nrg * (nblk // n_count), body, 0)

    rows_per = 4 * blk

    def combine(ci, carry):
        sl = pl.ds(pl.multiple_of(ci * rows_per, rows_per), rows_per)
        l0, l1, l2 = lbufs[0][sl, :], lbufs[1][sl, :], lbufs[2][sl, :]
        mx = jnp.maximum(jnp.maximum(l0, l1), l2)
        w0, w1, w2 = jnp.exp(l0 - mx), jnp.exp(l1 - mx), jnp.exp(l2 - mx)
        out = (obufs[0][sl, :] * w0 + obufs[1][sl, :] * w1 + obufs[2][sl, :] * w2) / (w0 + w1 + w2)
        o_ref[0, sl, :] = out.astype(o_ref.dtype)
        return carry

    lax.fori_loop(0, s // rows_per, combine, 0)


def _dilated_attention(q, k, v):
    bsz, nhp, s, _ = q.shape
    spec = pl.BlockSpec((1, 1, s, LANES), lambda bi, hp: (bi, hp, 0, 0))
    return pl.pallas_call(
        functools.partial(_dilated_body, s=s),
        grid=(bsz, nhp),
        in_specs=[spec, spec, spec],
        out_specs=pl.BlockSpec((1, s, LANES), lambda bi, hp: (bi, 0, hp)),
        out_shape=jax.ShapeDtypeStruct((bsz, s, SW_DIM), BF16),
        scratch_shapes=[pltpu.VMEM((s, LANES), F32)] * 6,
        compiler_params=_cparams("parallel", "parallel"),
        name="dilated_window_attention",
    )(q, k, v)


S5_PITCH = 40


def _s5_body(h_ref, toe_ref, wis_ref, wso_ref, apr_ref, api_ref, y_ref, xpad, ypad, ubuf, ybuf, *, ncb):
    l = S5_CHUNK
    gsz = S5_GROUP
    p = S5_STATE
    ngroups = LANES // gsz

    def copy_in(c, carry):
        xpad[pl.ds(pl.multiple_of(c * S5_PITCH, 8), l), :] = h_ref[0, pl.ds(pl.multiple_of(c * l, l), l), :]
        return carry

    lax.fori_loop(0, ncb, copy_in, 0, unroll=8)

    def gather(si, carry):
        ubuf[si] = xpad[pl.ds(si, ncb, stride=S5_PITCH), :].T.astype(BF16)
        return carry

    lax.fori_loop(0, l, gather, 0, unroll=4)

    lane = lax.broadcasted_iota(jnp.int32, (p, ncb), 1)
    nsteps = (ncb - 1).bit_length()
    for gi in range(ngroups):
        u = ubuf[:, gi * gsz:(gi + 1) * gsz, :].reshape(l * gsz, ncb)
        y_intra = _dot(toe_ref[gi], u)
        s_loc = _dot(wis_ref[gi], u)
        apr = apr_ref[gi]
        api = api_ref[gi]
        zr = s_loc[:p]
        zi = s_loc[p:]
        for kk in range(nsteps):
            sh = 1 << kk
            zr_s = pltpu.roll(zr, sh, 1)
            zi_s = pltpu.roll(zi, sh, 1)
            ar = apr[:, kk:kk + 1]
            ai = api[:, kk:kk + 1]
            ok = lane >= sh
            zr, zi = (zr + jnp.where(ok, ar * zr_s - ai * zi_s, 0.0),
                      zi + jnp.where(ok, ar * zi_s + ai * zr_s, 0.0))
        x_in = jnp.concatenate([jnp.where(lane >= 1, pltpu.roll(zr, 1, 1), 0.0),
                                jnp.where(lane >= 1, pltpu.roll(zi, 1, 1), 0.0)], axis=0)
        y = y_intra + _dot(wso_ref[gi], x_in.astype(BF16))
        ybuf[:, gi * gsz:(gi + 1) * gsz, :] = y.reshape(l, gsz, ncb)

    def scatter(si, carry):
        ypad[pl.ds(si, ncb, stride=S5_PITCH), :] = ybuf[si].T
        return carry

    lax.fori_loop(0, l, scatter, 0, unroll=4)

    def copy_out(c, carry):
        y_ref[0, pl.ds(pl.multiple_of(c * l, l), l), :] = (
            ypad[pl.ds(pl.multiple_of(c * S5_PITCH, 8), l), :].astype(y_ref.dtype))
        return carry

    lax.fori_loop(0, ncb, copy_out, 0, unroll=8)


def _s5_conv(h, toe, wis, wso, apr, api):
    bsz, s, d = h.shape
    l = S5_CHUNK
    ncb = s // l
    ngroups = LANES // S5_GROUP
    lg = l * S5_GROUP
    p2 = 2 * S5_STATE
    nk = apr.shape[-1]
    per_tile = lambda a, b: pl.BlockSpec((ngroups, a, b), lambda j, bi: (j, 0, 0))
    tok = pl.BlockSpec((1, s, LANES), lambda j, bi: (bi, 0, j))
    return pl.pallas_call(
        functools.partial(_s5_body, ncb=ncb),
        grid=(d // LANES, bsz),
        in_specs=[tok, per_tile(lg, lg), per_tile(p2, lg), per_tile(lg, p2),
                  per_tile(S5_STATE, nk), per_tile(S5_STATE, nk)],
        out_specs=tok,
        out_shape=jax.ShapeDtypeStruct((bsz, s, d), BF16),
        scratch_shapes=[pltpu.VMEM((ncb * S5_PITCH, LANES), F32), pltpu.VMEM((ncb * S5_PITCH, LANES), F32),
                        pltpu.VMEM((l, LANES, ncb), BF16), pltpu.VMEM((l, LANES, ncb), F32)],
        compiler_params=_cparams("parallel", "parallel"),
        name="s5_chunked_conv",
    )(h, toe, wis, wso, apr, api)


def _s5_tables(a_re, a_im, log_dt, b_re, b_im, c_re, c_im, ncb):
    l = S5_CHUNK
    a = lax.complex(a_re.astype(F32), a_im.astype(F32))
    dt = jnp.exp(log_dt.astype(F32))[:, None]
    adt = a * dt
    a_bar = jnp.exp(adt)
    b_bar = ((a_bar - 1.0) / a)[..., None] * lax.complex(b_re.astype(F32), b_im.astype(F32))
    cc = lax.complex(c_re.astype(F32), c_im.astype(F32))
    j = jnp.arange(l + 1, dtype=F32)
    apow = jnp.exp(adt[None] * j[:, None, None])
    kern = jnp.einsum('ghp,jgp,gpi->gjhi', cc, apow[:l], b_bar).real
    g = a.shape[0]
    kext = jnp.concatenate([kern[:, ::-1], jnp.zeros((g, l - 1, S5_GROUP, S5_GROUP), F32)], axis=1)
    kext = kext.transpose(0, 2, 1, 3).reshape(g, S5_GROUP, (2 * l - 1) * S5_GROUP).astype(BF16)
    toe = jnp.stack([kext[:, :, (l - 1 - t) * S5_GROUP:(2 * l - 1 - t) * S5_GROUP] for t in range(l)], axis=1)
    toe = toe.reshape(g, l * S5_GROUP, l * S5_GROUP)
    w_in = apow[:l][::-1].transpose(1, 2, 0)[..., None] * b_bar[:, :, None, :]
    w_in = w_in.reshape(g, S5_STATE, l * S5_GROUP)
    wis = jnp.concatenate([w_in.real, w_in.imag], axis=1)
    w_out = cc[:, None, :, :] * apow[1:l + 1].transpose(1, 0, 2)[:, :, None, :]
    w_out = w_out.reshape(g, l * S5_GROUP, S5_STATE)
    wso = jnp.concatenate([w_out.real, -w_out.imag], axis=2)
    nsteps = (ncb - 1).bit_length()
    kexp = (l * (2.0 ** jnp.arange(nsteps, dtype=F32)))
    ak = jnp.exp(adt[..., None] * kexp)
    return toe.astype(BF16), wis.astype(BF16), wso.astype(BF16), ak.real, ak.imag


def _even_mixer(h, cos, sin, w_in, conv_w, a_log, dt_bias, norm_g, w_out):
    w_gate = jnp.pad(w_in[:, _DNZ:_DNZ + 2 * DN_HEADS], ((0, 0), (0, GATE_PAD - 2 * DN_HEADS)))
    w_packed = jnp.concatenate([w_in[:, :_DNZ], w_gate, w_in[:, _DNZ + 2 * DN_HEADS:]], axis=1).astype(BF16)
    dn, gates, q, k, v = _hyb_in(h, w_packed, cos, sin)
    lane_pad = lambda p: jnp.pad(p.astype(F32), (DN_HEADS, GATE_PAD - 2 * DN_HEADS))[None, :]
    a_out = _deltanet(dn, gates, conv_w.astype(F32), lane_pad(a_log), lane_pad(dt_bias),
                      norm_g.astype(F32)[None, :])
    b_out = _dilated_attention(q, k, v)
    w16 = w_out.astype(BF16)
    return (a_out, b_out), (w16[:DN_KEY_DIM], w16[DN_KEY_DIM:])


def _odd_mixer(h, a_re, a_im, log_dt, b_re, b_im, c_re, c_im, d_skip, w_o, w_g):
    s = h.shape[1]
    toe, wis, wso, apr, api = _s5_tables(a_re, a_im, log_dt, b_re, b_im, c_re, c_im, s // S5_CHUNK)
    y = _s5_conv(h, toe, wis, wso, apr, api)
    return (y,), (d_skip.astype(F32)[None, :], w_o.astype(BF16), w_g.astype(BF16))


def kernel(x, mem, positions, hyb_w_in, dn_conv_w, dn_a_log, dn_dt_bias, dn_norm_g, hyb_w_out, s5_a_re, s5_a_im, s5_log_dt, s5_b_re, s5_b_im, s5_c_re, s5_c_im, s5_d, s5_glu_wo, s5_glu_wg, ln_mix_g, ln_mix_b, xq_w, xk_w, xv_w, xo_w, ln_x_g, ln_x_b, ffn_wg, ffn_wu, ffn_wd, ln_ffn_g, ln_ffn_b):
    bsz, s, d = x.shape
    m = mem.shape[1]
    depth = xq_w.shape[0]
    cos, sin = _rope_tables(positions)
    wkv_all = jnp.concatenate([jnp.concatenate([xk_w[l], xv_w[l]], axis=1) for l in range(depth)],
                              axis=1).astype(BF16)
    kv_all = _mem_kv(mem.reshape(bsz * m, d), wkv_all).reshape(bsz, m, depth * 2 * d)
    h = x
    for layer in range(depth):
        i = layer // 2
        even = layer % 2 == 0
        if even:
            mix_inputs, mix_weights = _even_mixer(h, cos, sin, hyb_w_in[i], dn_conv_w[i], dn_a_log[i],
                                                  dn_dt_bias[i], dn_norm_g[i], hyb_w_out[i])
        else:
            mix_inputs, mix_weights = _odd_mixer(h, s5_a_re[i], s5_a_im[i], s5_log_dt[i], s5_b_re[i], s5_b_im[i],
                                                 s5_c_re[i], s5_c_im[i], s5_d[i], s5_glu_wo[i], s5_glu_wg[i])
        ln = jnp.stack([ln_mix_g[layer], ln_mix_b[layer], ln_x_g[layer], ln_x_b[layer],
                        ln_ffn_g[layer], ln_ffn_b[layer]]).astype(F32)
        h = _layer_tail(h, mix_inputs, mix_weights, kv_all, layer, xq_w[layer].astype(BF16),
                        xo_w[layer].astype(BF16), ffn_wg[layer].astype(BF16), ffn_wu[layer].astype(BF16),
                        ffn_wd[layer].astype(BF16), ln, even=even)
    return h
```

```python
import functools
import math

import jax
import jax.numpy as jnp
from jax import lax
from jax.experimental import pallas as pl
from jax.experimental.pallas import tpu as pltpu

F32 = jnp.float32
BF16 = jnp.bfloat16

D_MODEL = 1024
DEPTH = 4
DN_HEADS = 4
DN_HEAD_DIM = 128
DN_KEY_DIM = DN_HEADS * DN_HEAD_DIM
DN_QKV_DIM = 3 * DN_KEY_DIM
DN_CONV = 4
DN_CHUNK = 128
SW_HEADS = 8
SW_HEAD_DIM = 64
SW_DIM = SW_HEADS * SW_HEAD_DIM
SW_BRANCHES = ((128, 1), (512, 4), (2048, 16))
SW_BLOCK = 128
ROPE_THETA = 10000.0
S5_GROUP = 16
S5_GROUPS = D_MODEL // S5_GROUP
S5_STATE = 64
S5_CHUNK = 32
X_HEADS = 4
X_HEAD_DIM = D_MODEL // X_HEADS
DEEPNORM_ALPHA = (2 * DEPTH) ** 0.25
LN_EPS = 1e-5
RMS_EPS = 1e-6

LANES = 128
GATE_PAD = LANES
VMEM_LIMIT = 56 * 1024 * 1024
NEG_BIG = -1e30

TM = 512
DN_TILE = 256
DIL_UNROLL = 4


def _cparams(*sem):
    return pltpu.CompilerParams(dimension_semantics=sem, vmem_limit_bytes=VMEM_LIMIT)


def _const_spec(shape):
    nd = len(shape)
    return pl.BlockSpec(shape, lambda *_: (0,) * nd, pipeline_mode=pl.Buffered(1))


def _dot(a, b):
    return jnp.dot(a, b, preferred_element_type=F32)


def _dot_nt(a, b):
    return lax.dot_general(a, b, (((1,), (1,)), ((), ())), preferred_element_type=F32)


def _dot_tn(a, b):
    return lax.dot_general(a, b, (((0,), (0,)), ((), ())), preferred_element_type=F32)


def _post_norm(h, sub, g, b):
    y = DEEPNORM_ALPHA * h + sub
    mu = jnp.mean(y, axis=-1, keepdims=True)
    yc = y - mu
    var = jnp.mean(yc * yc, axis=-1, keepdims=True)
    return yc * lax.rsqrt(var + LN_EPS) * g + b


def _sigmoid(x):
    return 1.0 / (1.0 + jnp.exp(-x))


def _silu(x):
    return x * _sigmoid(x)


FFN_CHUNK = 256


def _ffn_sub(x, wg_ref, wu_ref, wd_ref):
    xb = x.astype(BF16)
    acts = []
    for lo in range(0, wg_ref.shape[1], FFN_CHUNK):
        hi = lo + FFN_CHUNK
        gate = _dot(xb, wg_ref[:, lo:hi])
        up = _dot(xb, wu_ref[:, lo:hi])
        acts.append((_silu(gate) * up).astype(BF16))
    return _dot(jnp.concatenate(acts, axis=-1), wd_ref[...])


def _mm_body(x_ref, w_ref, o_ref):
    o_ref[...] = _dot(x_ref[...].astype(BF16), w_ref[...]).astype(o_ref.dtype)


def _mem_kv(mem2, wkv_all):
    r, d = mem2.shape
    n = wkv_all.shape[1]
    nb = 2 * d
    tm = min(TM, r)
    return pl.pallas_call(
        _mm_body,
        grid=(n // nb, r // tm),
        in_specs=[pl.BlockSpec((tm, d), lambda j, i: (i, 0)),
                  pl.BlockSpec((d, nb), lambda j, i: (0, j))],
        out_specs=pl.BlockSpec((tm, nb), lambda j, i: (i, j)),
        out_shape=jax.ShapeDtypeStruct((r, n), BF16),
        compiler_params=_cparams("parallel", "parallel"),
        name="memory_kv_proj",
    )(mem2, wkv_all)


def _xattn_sub(x, kv_ref, wq_ref, wo_ref):
    d = x.shape[-1]
    q = _dot(x.astype(BF16), wq_ref[...]) * (X_HEAD_DIM ** -0.5)
    qb = q.astype(BF16)
    outs = []
    for hh in range(X_HEADS):
        lo, hi = hh * X_HEAD_DIM, (hh + 1) * X_HEAD_DIM
        k = kv_ref[0, :, lo:hi]
        v = kv_ref[0, :, d + lo:d + hi]
        s = _dot_nt(qb[:, lo:hi], k)
        m = jnp.max(s, axis=-1, keepdims=True)
        p = jnp.exp(s - m)
        l = jnp.sum(p, axis=-1, keepdims=True)
        outs.append(_dot(p.astype(BF16), v) * (1.0 / l))
    o = jnp.concatenate(outs, axis=-1).astype(BF16)
    return _dot(o, wo_ref[...])


def _gelu_tanh(y):
    return y * (0.5 * (1.0 + jnp.tanh(math.sqrt(2.0 / math.pi) * (y + 0.044715 * (y * y * y)))))


def _tail_body(*refs, even):
    if even:
        (h_ref, a_ref, b_ref, wa_ref, wb_ref, kv_ref, wq_ref, wo_ref, wg_ref, wu_ref, wd_ref, ln_ref, o_ref) = refs
        x = h_ref[0]
        sub = _dot(a_ref[0], wa_ref[...]) + _dot(b_ref[0], wb_ref[...])
    else:
        (h_ref, y_ref, dsk_ref, gwo_ref, gwg_ref, kv_ref, wq_ref, wo_ref, wg_ref, wu_ref, wd_ref, ln_ref,
         o_ref) = refs
        x = h_ref[0]
        hid = _gelu_tanh(y_ref[0].astype(F32) + dsk_ref[...] * x).astype(BF16)
        sub = _dot(hid, gwo_ref[...]) * _sigmoid(_dot(hid, gwg_ref[...]))
    x = _post_norm(x, sub, ln_ref[0:1, :], ln_ref[1:2, :])
    x = _post_norm(x, _xattn_sub(x, kv_ref, wq_ref, wo_ref), ln_ref[2:3, :], ln_ref[3:4, :])
    o_ref[0] = _post_norm(x, _ffn_sub(x, wg_ref, wu_ref, wd_ref), ln_ref[4:5, :], ln_ref[5:6, :])


def _layer_tail(h, mix_inputs, mix_weights, kv_all, layer, wq, wo, wg, wu, wd, ln, *, even):
    bsz, s, d = h.shape
    m = kv_all.shape[1]
    f = wg.shape[2]
    tm = min(TM, s)
    tok = lambda w: pl.BlockSpec((1, tm, w), lambda bi, i: (bi, i, 0))
    layer_w = lambda a, b: pl.BlockSpec((None, a, b), lambda bi, i: (layer, 0, 0), pipeline_mode=pl.Buffered(1))
    in_specs = ([tok(d)] + [tok(t.shape[-1]) for t in mix_inputs] + [_const_spec(w.shape) for w in mix_weights]
                + [pl.BlockSpec((1, m, 2 * d), lambda bi, i: (bi, 0, layer)),
                   layer_w(d, d), layer_w(d, d),
                   layer_w(d, f), layer_w(d, f), layer_w(f, d), _const_spec(ln.shape)])
    return pl.pallas_call(
        functools.partial(_tail_body, even=even),
        grid=(bsz, s // tm),
        in_specs=in_specs,
        out_specs=tok(d),
        out_shape=jax.ShapeDtypeStruct((bsz, s, d), F32),
        compiler_params=_cparams("parallel", "parallel"),
        name="mixer_out_xattn_swiglu",
    )(h, *mix_inputs, *mix_weights, kv_all, wq, wo, wg, wu, wd, ln)


def _rope_table_body(pos_ref, invf_ref, sgn_ref, cos_ref, sin_ref):
    ang = pos_ref[0] * invf_ref[...]
    cos_ref[0] = jnp.cos(ang)
    sin_ref[0] = jnp.sin(ang) * sgn_ref[...]


def _rope_tables(positions):
    bsz, s = positions.shape
    half = SW_HEAD_DIM // 2
    inv_freq = ROPE_THETA ** (-jnp.arange(0, SW_HEAD_DIM, 2, dtype=F32) / SW_HEAD_DIM)
    invf = jnp.tile(inv_freq, LANES // half)[None, :]
    sgn = jnp.tile(jnp.concatenate([-jnp.ones((half,), F32), jnp.ones((half,), F32)]),
                   LANES // SW_HEAD_DIM)[None, :]
    posf = positions.astype(F32)[..., None]
    tm = min(TM, s)
    return pl.pallas_call(
        _rope_table_body,
        grid=(bsz, s // tm),
        in_specs=[pl.BlockSpec((1, tm, 1), lambda bi, i: (bi, i, 0)),
                  _const_spec((1, LANES)), _const_spec((1, LANES))],
        out_specs=[pl.BlockSpec((1, tm, LANES), lambda bi, i: (bi, i, 0))] * 2,
        out_shape=[jax.ShapeDtypeStruct((bsz, s, LANES), F32)] * 2,
        compiler_params=_cparams("parallel", "parallel"),
        name="rope_tables",
    )(posf, invf, sgn)


_DNZ = DN_QKV_DIM + DN_KEY_DIM
_SW_OFF = _DNZ + GATE_PAD
_HYB_PACKED = _SW_OFF + 3 * SW_DIM


def _hyb_in_body(h_ref, w_ref, cos_ref, sin_ref, dn_ref, gate_ref, q_ref, k_ref, v_ref):
    x = h_ref[0].astype(BF16)
    dn_ref[0] = _dot(x, w_ref[:, :_DNZ]).astype(dn_ref.dtype)
    gate_ref[0] = _dot(x, w_ref[:, _DNZ:_SW_OFF])
    cos = cos_ref[0]
    sin = sin_ref[0]
    lane = lax.broadcasted_iota(jnp.int32, cos.shape, 1)
    first_half = (lane % SW_HEAD_DIM) < (SW_HEAD_DIM // 2)
    half = SW_HEAD_DIM // 2
    for idx, out_ref in enumerate((q_ref, k_ref, v_ref)):
        lo = _SW_OFF + idx * SW_DIM
        y = _dot(x, w_ref[:, lo:lo + SW_DIM])
        for hp in range(SW_DIM // LANES):
            yy = y[:, hp * LANES:(hp + 1) * LANES]
            if idx < 2:
                swapped = jnp.where(first_half, pltpu.roll(yy, LANES - half, 1), pltpu.roll(yy, half, 1))
                yy = yy * cos + swapped * sin
            out_ref[0, hp] = yy


def _hyb_in(h, w_packed, cos, sin):
    bsz, s, d = h.shape
    tm = min(TM, s)
    nhp = SW_DIM // LANES
    qkv_shape = jax.ShapeDtypeStruct((bsz, nhp, s, LANES), F32)
    qkv_spec = pl.BlockSpec((1, nhp, tm, LANES), lambda bi, i: (bi, 0, i, 0))
    tok = lambda w: pl.BlockSpec((1, tm, w), lambda bi, i: (bi, i, 0))
    return pl.pallas_call(
        _hyb_in_body,
        grid=(bsz, s // tm),
        in_specs=[tok(d), _const_spec((d, _HYB_PACKED)), tok(LANES), tok(LANES)],
        out_specs=[tok(_DNZ), tok(GATE_PAD), qkv_spec, qkv_spec, qkv_spec],
        out_shape=[jax.ShapeDtypeStruct((bsz, s, _DNZ), BF16),
                   jax.ShapeDtypeStruct((bsz, s, GATE_PAD), F32),
                   qkv_shape, qkv_shape, qkv_shape],
        compiler_params=_cparams("parallel", "parallel"),
        name="hybrid_in_proj_rope",
    )(h, w_packed, cos, sin)


def _l2n(x):
    return x * lax.rsqrt(jnp.sum(x * x, axis=-1, keepdims=True) + RMS_EPS)


def _deltanet_body(dn_ref, gate_ref, convw_ref, alog_ref, dtb_ref, ng_ref, o_ref, halo, state, *, ts):
    i = pl.program_id(1)
    c = DN_CHUNK
    nchunk = ts // c
    hd = DN_HEAD_DIM

    @pl.when(i == 0)
    def _():
        halo[1] = jnp.zeros(halo.shape[1:], halo.dtype)
        state[...] = jnp.zeros(state.shape, F32)

    x = dn_ref[0]
    xq = x[:, :DN_QKV_DIM]
    row = lax.broadcasted_iota(jnp.int32, (ts, ts), 0)
    col = lax.broadcasted_iota(jnp.int32, (ts, ts), 1)
    nh = halo.shape[1]
    xext = jnp.concatenate([halo[(i + 1) % 2], xq], axis=0)
    ntap = DN_CONV - 1
    erow = lax.broadcasted_iota(jnp.int32, (ntap * ts, nh + ts), 0)
    ecol = lax.broadcasted_iota(jnp.int32, (ntap * ts, nh + ts), 1)
    sel = jnp.where(ecol == (erow % ts) + (nh - ntap) + erow // ts, 1.0, 0.0).astype(BF16)
    shifted = _dot(sel, xext)
    y = convw_ref[ntap:DN_CONV, :] * xq.astype(F32)
    for j in range(ntap):
        y = y + convw_ref[j:j + 1, :] * shifted[j * ts:(j + 1) * ts]
    halo[i % 2] = xq[ts - nh:, :]
    y = _silu(y)

    gt = gate_ref[0]
    beta_all = _sigmoid(gt)
    sp = gt + dtb_ref[...]
    softplus = jnp.maximum(sp, 0.0) + jnp.log(1.0 + jnp.exp(-jnp.abs(sp)))
    g_all = -jnp.exp(alog_ref[...]) * softplus

    same_chunk = (row // c) == (col // c)
    gcum = jnp.dot((same_chunk & (col <= row)).astype(F32), g_all, preferred_element_type=F32,
                   precision=lax.Precision.HIGHEST)
    gtot = jnp.dot(same_chunk.astype(F32), g_all, preferred_element_type=F32, precision=lax.Precision.HIGHEST)
    gcum_t = lax.dot_general(g_all, (same_chunk & (row <= col)).astype(F32), (((0,), (0,)), ((), ())),
                             preferred_element_type=F32, precision=lax.Precision.HIGHEST)
    crow = lax.broadcasted_iota(jnp.int32, (c, c), 0)
    ccol = lax.broadcasted_iota(jnp.int32, (c, c), 1)
    causal = ccol <= crow
    strict = ccol < crow
    eye = (crow == ccol).astype(F32)

    heads = range(DN_HEADS)
    pre = {}
    for h in heads:
        q = _l2n(y[:, h * hd:(h + 1) * hd]) * (hd ** -0.5)
        k = _l2n(y[:, DN_KEY_DIM + h * hd:DN_KEY_DIM + (h + 1) * hd])
        v = y[:, 2 * DN_KEY_DIM + h * hd:2 * DN_KEY_DIM + (h + 1) * hd]
        beta = beta_all[:, h:h + 1]
        gcol = gcum[:, DN_HEADS + h:DN_HEADS + h + 1]
        gl = gtot[:, DN_HEADS + h:DN_HEADS + h + 1]
        kb = k * beta
        egc = jnp.exp(gcol)
        rhs = jnp.concatenate([v * beta, kb * egc], axis=-1).astype(BF16)
        q_dec = q * egc
        k_dec = (k * jnp.exp(gl - gcol)).astype(BF16)
        q16, k16, kb16 = q.astype(BF16), k.astype(BF16), kb.astype(BF16)
        for n in range(nchunk):
            r0, r1 = n * c, (n + 1) * c
            grow = gcum_t[DN_HEADS + h:DN_HEADS + h + 1, r0:r1]
            decay = jnp.exp(jnp.where(causal, gcol[r0:r1] - grow, NEG_BIG))
            low = jnp.where(strict, _dot_nt(kb16[r0:r1], k16[r0:r1]), 0.0) * decay
            a16 = (_dot_nt(q16[r0:r1], k16[r0:r1]) * decay).astype(BF16)
            pre[h, n] = (low, rhs[r0:r1], q_dec[r0:r1], k_dec[r0:r1], a16, jnp.exp(gl[r0:r0 + 1, :]))

    keys = list(pre)
    half = c // 2
    diag_half = (crow // half) == (ccol // half)
    lows_d = {kk: jnp.where(diag_half, pre[kk][0], 0.0) for kk in keys}
    lows_o = {kk: jnp.where(diag_half, 0.0, pre[kk][0]).astype(BF16) for kk in keys}
    invs = {kk: eye - lows_d[kk] for kk in keys}
    pws = {kk: _dot(lows_d[kk].astype(BF16), lows_d[kk].astype(BF16)) for kk in keys}
    nfac = (half - 1).bit_length() - 1
    for it in range(nfac):
        pw16 = {kk: pws[kk].astype(BF16) for kk in keys}
        invs = {kk: invs[kk] + _dot(invs[kk].astype(BF16), pw16[kk]) for kk in keys}
        if it < nfac - 1:
            pws = {kk: _dot(pw16[kk], pw16[kk]) for kk in keys}
    inv16 = {kk: invs[kk].astype(BF16) for kk in keys}
    cross = {kk: _dot(inv16[kk], lows_o[kk]).astype(BF16) for kk in keys}
    invs = {kk: invs[kk] - _dot(cross[kk], inv16[kk]) for kk in keys}
    sols = {kk: _dot(invs[kk].astype(BF16), pre[kk][1]).astype(BF16) for kk in keys}
    ksol = {kk: _dot_tn(pre[kk][3], sols[kk]) for kk in keys}
    asol = {kk: _dot(pre[kk][4], sols[kk]) for kk in keys}
    qeff = {kk: (pre[kk][2] - asol[kk][:, hd:]).astype(BF16) for kk in keys}

    sts = [state[h] for h in heads]
    outs = [[] for _ in heads]
    for n in range(nchunk):
        for h in heads:
            st16 = sts[h].astype(BF16)
            outs[h].append(_dot(qeff[h, n], st16) + asol[h, n][:, :hd])
            sts[h] = (sts[h] * pre[h, n][5] + ksol[h, n][:, :hd]) - _dot(ksol[h, n][:, hd:].astype(BF16), st16)
    for h in heads:
        state[h] = sts[h]
        o = jnp.concatenate(outs[h], axis=0)
        o = o * lax.rsqrt(jnp.mean(o * o, axis=-1, keepdims=True) + RMS_EPS) * ng_ref[...]
        z = x[:, DN_QKV_DIM + h * hd:DN_QKV_DIM + (h + 1) * hd].astype(F32)
        o_ref[0, :, h * hd:(h + 1) * hd] = (o * _silu(z)).astype(o_ref.dtype)


def _deltanet(dn, gates, conv_w, alog_row, dtb_row, norm_g):
    bsz, s, _ = dn.shape
    ts = min(DN_TILE, s)
    tok = lambda w: pl.BlockSpec((1, ts, w), lambda bi, i: (bi, i, 0))
    return pl.pallas_call(
        functools.partial(_deltanet_body, ts=ts),
        grid=(bsz, s // ts),
        in_specs=[tok(_DNZ), tok(GATE_PAD), _const_spec((DN_CONV, DN_QKV_DIM)),
                  _const_spec((1, GATE_PAD)), _const_spec((1, GATE_PAD)), _const_spec((1, DN_HEAD_DIM))],
        out_specs=tok(DN_KEY_DIM),
        out_shape=jax.ShapeDtypeStruct((bsz, s, DN_KEY_DIM), BF16),
        scratch_shapes=[pltpu.VMEM((2, LANES, DN_QKV_DIM), BF16),
                        pltpu.VMEM((DN_HEADS, DN_HEAD_DIM, DN_HEAD_DIM), F32)],
        compiler_params=_cparams("parallel", "arbitrary"),
        name="gated_deltanet",
    )(dn, gates, conv_w, alog_row, dtb_row, norm_g)


def _dilated_body(q_ref, k_ref, v_ref, o_ref, *bufs, s):
    blk = SW_BLOCK
    obufs, lbufs = bufs[:3], bufs[3:]
    lane = lax.broadcasted_iota(jnp.int32, (blk, LANES), 1)
    head0 = lane < SW_HEAD_DIM
    qi = lax.broadcasted_iota(jnp.int32, (2 * blk, 2 * blk), 0) % blk
    kj = lax.broadcasted_iota(jnp.int32, (2 * blk, 2 * blk), 1)
    dist = blk + qi - kj
    scale = SW_HEAD_DIM ** -0.5

    for br, (window, r) in enumerate(SW_BRANCHES):
        steps = window // r
        band = (dist >= 0) & (dist <= steps)
        nblk = s // (r * blk)
        n_count = min(DIL_UNROLL, nblk)
        rho_count = DIL_UNROLL // n_count
        nrg = r // rho_count

        def rows(start, r=r):
            if r == 1:
                return pl.ds(pl.multiple_of(start, blk), blk)
            return pl.ds(start, blk, stride=r)

        def body(it, carry, r=r, band=band, br=br, rows=rows, n_count=n_count, rho_count=rho_count, nrg=nrg):
            rho_base = (it & (nrg - 1)) * rho_count
            n0 = (it >> (nrg.bit_length() - 1)) * n_count
            starts, kcats, vcats, firsts = [], [], [], []
            for dr in range(rho_count):
                base = rho_base + dr + (r * blk) * n0
                pstart = jnp.maximum(base - r * blk, 0)
                kprev = k_ref[0, 0, rows(pstart), :].astype(BF16)
                vprev = v_ref[0, 0, rows(pstart), :].astype(BF16)
                for dn in range(n_count):
                    start = base + (r * blk) * dn
                    kcur = k_ref[0, 0, rows(start), :].astype(BF16)
                    vcur = v_ref[0, 0, rows(start), :].astype(BF16)
                    starts.append(start)
                    kcats.append(jnp.concatenate([kprev, kcur], axis=0))
                    vcats.append(jnp.concatenate([vprev, vcur], axis=0))
                    firsts.append(dn == 0)
                    kprev, vprev = kcur, vcur
            scs = []
            for start, kcat in zip(starts, kcats):
                qb = q_ref[0, 0, rows(start), :] * scale
                q2 = jnp.concatenate([jnp.where(head0, qb, 0.0), jnp.where(head0, 0.0, qb)], axis=0)
                scs.append(_dot_nt(q2.astype(BF16), kcat))
            first_valid = band & ((n0 > 0) | (kj >= blk))
            scs = [jnp.where(first_valid if first else band, sc, NEG_BIG) for sc, first in zip(scs, firsts)]
            ms = [jnp.max(sc, axis=-1, keepdims=True) for sc in scs]
            ps = [jnp.exp(sc - m) for sc, m in zip(scs, ms)]
            ls = [jnp.sum(p, axis=-1, keepdims=True) for p in ps]
            o2s = [_dot(p.astype(BF16), vcat) * (1.0 / l) for p, vcat, l in zip(ps, vcats, ls)]
            for start, o2, m, l in zip(starts, o2s, ms, ls):
                lse2 = m + jnp.log(l)
                obufs[br][rows(start), :] = jnp.where(head0, o2[:blk], o2[blk:])
                lbufs[br][rows(start), :] = jnp.where(head0, lse2[:blk], lse2[blk:])
            return carry

        lax.fori_loop(0, nrg * (nblk // n_count), body, 0)

    rows_per = 4 * blk

    def combine(ci, carry):
        sl = pl.ds(pl.multiple_of(ci * rows_per, rows_per), rows_per)
        l0, l1, l2 = lbufs[0][sl, :], lbufs[1][sl, :], lbufs[2][sl, :]
        mx = jnp.maximum(jnp.maximum(l0, l1), l2)
        w0, w1, w2 = jnp.exp(l0 - mx), jnp.exp(l1 - mx), jnp.exp(l2 - mx)
        out = (obufs[0][sl, :] * w0 + obufs[1][sl, :] * w1 + obufs[2][sl, :] * w2) / (w0 + w1 + w2)
        o_ref[0, sl, :] = out.astype(o_ref.dtype)
        return carry

    lax.fori_loop(0, s // rows_per, combine, 0)


def _dilated_attention(q, k, v):
    bsz, nhp, s, _ = q.shape
    spec = pl.BlockSpec((1, 1, s, LANES), lambda bi, hp: (bi, hp, 0, 0))
    return pl.pallas_call(
        functools.partial(_dilated_body, s=s),
        grid=(bsz, nhp),
        in_specs=[spec, spec, spec],
        out_specs=pl.BlockSpec((1, s, LANES), lambda bi, hp: (bi, 0, hp)),
        out_shape=jax.ShapeDtypeStruct((bsz, s, SW_DIM), BF16),
        scratch_shapes=[pltpu.VMEM((s, LANES), F32)] * 6,
        compiler_params=_cparams("parallel", "parallel"),
        name="dilated_window_attention",
    )(q, k, v)


S5_PITCH = 40


def _s5_body(h_ref, toe_ref, wis_ref, wso_ref, apr_ref, api_ref, y_ref, xpad, ypad, ubuf, ybuf, *, ncb):
    l = S5_CHUNK
    gsz = S5_GROUP
    p = S5_STATE
    ngroups = LANES // gsz

    def copy_in(c, carry):
        xpad[pl.ds(pl.multiple_of(c * S5_PITCH, 8), l), :] = h_ref[0, pl.ds(pl.multiple_of(c * l, l), l), :]
        return carry

    lax.fori_loop(0, ncb, copy_in, 0, unroll=8)

    def gather(si, carry):
        ubuf[si] = xpad[pl.ds(si, ncb, stride=S5_PITCH), :].T.astype(BF16)
        return carry

    lax.fori_loop(0, l, gather, 0, unroll=4)

    lane = lax.broadcasted_iota(jnp.int32, (p, ncb), 1)
    nsteps = (ncb - 1).bit_length()
    for gi in range(ngroups):
        u = ubuf[:, gi * gsz:(gi + 1) * gsz, :].reshape(l * gsz, ncb)
        y_intra = _dot(toe_ref[gi], u)
        s_loc = _dot(wis_ref[gi], u)
        apr = apr_ref[gi]
        api = api_ref[gi]
        zr = s_loc[:p]
        zi = s_loc[p:]
        for kk in range(nsteps):
            sh = 1 << kk
            zr_s = pltpu.roll(zr, sh, 1)
            zi_s = pltpu.roll(zi, sh, 1)
            ar = apr[:, kk:kk + 1]
            ai = api[:, kk:kk + 1]
            ok = lane >= sh
            zr, zi = (zr + jnp.where(ok, ar * zr_s - ai * zi_s, 0.0),
                      zi + jnp.where(ok, ar * zi_s + ai * zr_s, 0.0))
        x_in = jnp.concatenate([jnp.where(lane >= 1, pltpu.roll(zr, 1, 1), 0.0),
                                jnp.where(lane >= 1, pltpu.roll(zi, 1, 1), 0.0)], axis=0)
        y = y_intra + _dot(wso_ref[gi], x_in.astype(BF16))
        ybuf[:, gi * gsz:(gi + 1) * gsz, :] = y.reshape(l, gsz, ncb)

    def scatter(si, carry):
        ypad[pl.ds(si, ncb, stride=S5_PITCH), :] = ybuf[si].T
        return carry

    lax.fori_loop(0, l, scatter, 0, unroll=4)

    def copy_out(c, carry):
        y_ref[0, pl.ds(pl.multiple_of(c * l, l), l), :] = (
            ypad[pl.ds(pl.multiple_of(c * S5_PITCH, 8), l), :].astype(y_ref.dtype))
        return carry

    lax.fori_loop(0, ncb, copy_out, 0, unroll=8)


def _s5_conv(h, toe, wis, wso, apr, api):
    bsz, s, d = h.shape
    l = S5_CHUNK
    ncb = s // l
    ngroups = LANES // S5_GROUP
    lg = l * S5_GROUP
    p2 = 2 * S5_STATE
    nk = apr.shape[-1]
    per_tile = lambda a, b: pl.BlockSpec((ngroups, a, b), lambda j, bi: (j, 0, 0))
    tok = pl.BlockSpec((1, s, LANES), lambda j, bi: (bi, 0, j))
    return pl.pallas_call(
        functools.partial(_s5_body, ncb=ncb),
        grid=(d // LANES, bsz),
        in_specs=[tok, per_tile(lg, lg), per_tile(p2, lg), per_tile(lg, p2),
                  per_tile(S5_STATE, nk), per_tile(S5_STATE, nk)],
        out_specs=tok,
        out_shape=jax.ShapeDtypeStruct((bsz, s, d), BF16),
        scratch_shapes=[pltpu.VMEM((ncb * S5_PITCH, LANES), F32), pltpu.VMEM((ncb * S5_PITCH, LANES), F32),
                        pltpu.VMEM((l, LANES, ncb), BF16), pltpu.VMEM((l, LANES, ncb), F32)],
        compiler_params=_cparams("parallel", "parallel"),
        name="s5_chunked_conv",
    )(h, toe, wis, wso, apr, api)


def _s5_tables(a_re, a_im, log_dt, b_re, b_im, c_re, c_im, ncb):
    l = S5_CHUNK
    a = lax.complex(a_re.astype(F32), a_im.astype(F32))
    dt = jnp.exp(log_dt.astype(F32))[:, None]
    adt = a * dt
    a_bar = jnp.exp(adt)
    b_bar = ((a_bar - 1.0) / a)[..., None] * lax.complex(b_re.astype(F32), b_im.astype(F32))
    cc = lax.complex(c_re.astype(F32), c_im.astype(F32))
    j = jnp.arange(l + 1, dtype=F32)
    apow = jnp.exp(adt[None] * j[:, None, None])
    kern = jnp.einsum('ghp,jgp,gpi->gjhi', cc, apow[:l], b_bar).real
    g = a.shape[0]
    kext = jnp.concatenate([kern[:, ::-1], jnp.zeros((g, l - 1, S5_GROUP, S5_GROUP), F32)], axis=1)
    kext = kext.transpose(0, 2, 1, 3).reshape(g, S5_GROUP, (2 * l - 1) * S5_GROUP).astype(BF16)
    toe = jnp.stack([kext[:, :, (l - 1 - t) * S5_GROUP:(2 * l - 1 - t) * S5_GROUP] for t in range(l)], axis=1)
    toe = toe.reshape(g, l * S5_GROUP, l * S5_GROUP)
    w_in = apow[:l][::-1].transpose(1, 2, 0)[..., None] * b_bar[:, :, None, :]
    w_in = w_in.reshape(g, S5_STATE, l * S5_GROUP)
    wis = jnp.concatenate([w_in.real, w_in.imag], axis=1)
    w_out = cc[:, None, :, :] * apow[1:l + 1].transpose(1, 0, 2)[:, :, None, :]
    w_out = w_out.reshape(g, l * S5_GROUP, S5_STATE)
    wso = jnp.concatenate([w_out.real, -w_out.imag], axis=2)
    nsteps = (ncb - 1).bit_length()
    kexp = (l * (2.0 ** jnp.arange(nsteps, dtype=F32)))
    ak = jnp.exp(adt[..., None] * kexp)
    return toe.astype(BF16), wis.astype(BF16), wso.astype(BF16), ak.real, ak.imag


def _even_mixer(h, cos, sin, w_in, conv_w, a_log, dt_bias, norm_g, w_out):
    w_gate = jnp.pad(w_in[:, _DNZ:_DNZ + 2 * DN_HEADS], ((0, 0), (0, GATE_PAD - 2 * DN_HEADS)))
    w_packed = jnp.concatenate([w_in[:, :_DNZ], w_gate, w_in[:, _DNZ + 2 * DN_HEADS:]], axis=1).astype(BF16)
    dn, gates, q, k, v = _hyb_in(h, w_packed, cos, sin)
    lane_pad = lambda p: jnp.pad(p.astype(F32), (DN_HEADS, GATE_PAD - 2 * DN_HEADS))[None, :]
    a_out = _deltanet(dn, gates, conv_w.astype(F32), lane_pad(a_log), lane_pad(dt_bias),
                      norm_g.astype(F32)[None, :])
    b_out = _dilated_attention(q, k, v)
    w16 = w_out.astype(BF16)
    return (a_out, b_out), (w16[:DN_KEY_DIM], w16[DN_KEY_DIM:])


def _odd_mixer(h, a_re, a_im, log_dt, b_re, b_im, c_re, c_im, d_skip, w_o, w_g):
    s = h.shape[1]
    toe, wis, wso, apr, api = _s5_tables(a_re, a_im, log_dt, b_re, b_im, c_re, c_im, s // S5_CHUNK)
    y = _s5_conv(h, toe, wis, wso, apr, api)
    return (y,), (d_skip.astype(F32)[None, :], w_o.astype(BF16), w_g.astype(BF16))


def kernel(x, mem, positions, hyb_w_in, dn_conv_w, dn_a_log, dn_dt_bias, dn_norm_g, hyb_w_out, s5_a_re, s5_a_im, s5_log_dt, s5_b_re, s5_b_im, s5_c_re, s5_c_im, s5_d, s5_glu_wo, s5_glu_wg, ln_mix_g, ln_mix_b, xq_w, xk_w, xv_w, xo_w, ln_x_g, ln_x_b, ffn_wg, ffn_wu, ffn_wd, ln_ffn_g, ln_ffn_b):
    bsz, s, d = x.shape
    m = mem.shape[1]
    depth = xq_w.shape[0]
    cos, sin = _rope_tables(positions)
    wkv_all = jnp.concatenate([jnp.concatenate([xk_w[l], xv_w[l]], axis=1) for l in range(depth)],
                              axis=1).astype(BF16)
    kv_all = _mem_kv(mem.reshape(bsz * m, d), wkv_all).reshape(bsz, m, depth * 2 * d)
    tail_w = [w.astype(BF16) for w in (xq_w, xo_w, ffn_wg, ffn_wu, ffn_wd)]
    h = x
    for layer in range(depth):
        i = layer // 2
        even = layer % 2 == 0
        if even:
            mix_inputs, mix_weights = _even_mixer(h, cos, sin, hyb_w_in[i], dn_conv_w[i], dn_a_log[i],
                                                  dn_dt_bias[i], dn_norm_g[i], hyb_w_out[i])
        else:
            mix_inputs, mix_weights = _odd_mixer(h, s5_a_re[i], s5_a_im[i], s5_log_dt[i], s5_b_re[i], s5_b_im[i],
                                                 s5_c_re[i], s5_c_im[i], s5_d[i], s5_glu_wo[i], s5_glu_wg[i])
        ln = jnp.stack([ln_mix_g[layer], ln_mix_b[layer], ln_x_g[layer], ln_x_b[layer],
                        ln_ffn_g[layer], ln_ffn_b[layer]]).astype(F32)
        h = _layer_tail(h, mix_inputs, mix_weights, kv_all, layer, *tail_w, ln, even=even)
    return h
```
